```python
import math
import jax, jax.numpy as jnp
from jax import lax
import numpy as np

D_MODEL = 2048
BATCH = 2
SEQ = 4096
DEPTH = 1

HEAD_DIM = 64
SB_HEADS = 16
SWA_Q_HEADS = 16
SWA_KV_HEADS = 2
SWA_GROUP = SWA_Q_HEADS // SWA_KV_HEADS
WINDOW = 128
BLOCK = 128
D_FF = 5632
CONV_WIDTH = 3
RMS_EPS = 1e-5
NEG_INF = -1e30

SB_W = SB_HEADS * HEAD_DIM
SWA_Q_W = SWA_Q_HEADS * HEAD_DIM
SWA_KV_W = SWA_KV_HEADS * HEAD_DIM
IN_SIZES = [SB_W, SB_W, SB_W, SWA_Q_W, SWA_KV_W, SWA_KV_W, D_MODEL, D_MODEL]
IN_W = sum(IN_SIZES)
IN_SPLITS = [int(v) for v in np.cumsum(IN_SIZES)[:-1]]

kernel_name = "hybrid_stickbreak_swa_sink_convffn"


def rmsnorm(x, g):
    xf = x.astype(jnp.float32)
    y = xf * lax.rsqrt(jnp.mean(xf * xf, axis=-1, keepdims=True) + RMS_EPS)
    return (y * g.astype(jnp.float32)).astype(x.dtype)


def alibi_slopes(n_heads):
    return jnp.asarray(np.power(2.0, -8.0 * np.arange(1, n_heads + 1) / n_heads).astype(np.float32))


def stick_breaking_attention(q, k, v):
    b, s, h, dh = q.shape
    nblk = s // BLOCK
    scale = 1.0 / math.sqrt(dh)
    qb = q.reshape(b, nblk, BLOCK, h, dh).transpose(1, 0, 3, 2, 4)
    kt = k.transpose(0, 2, 1, 3)
    vt = v.transpose(0, 2, 1, 3)
    key_pos = jnp.arange(s)

    def one_block(args):
        qi, i = args
        z = jnp.einsum('bhqd,bhkd->bhqk', qi, kt).astype(jnp.float32) * scale
        q_pos = i * BLOCK + jnp.arange(BLOCK)
        causal = key_pos[None, :] < q_pos[:, None]
        log_beta = jax.nn.log_sigmoid(z)
        log_one_minus = jnp.where(causal, jax.nn.log_sigmoid(-z), 0.0)
        suffix = lax.cumsum(log_one_minus, axis=3, reverse=True) - log_one_minus
        a = jnp.where(causal, jnp.exp(log_beta + suffix), 0.0)
        return jnp.einsum('bhqk,bhkd->bhqd', a.astype(vt.dtype), vt)

    out = lax.map(one_block, (qb, jnp.arange(nblk)))
    return out.transpose(1, 0, 3, 2, 4).reshape(b, s, h * dh)


def sliding_window_gqa(q, k, v, sinks, slopes):
    b, s, hq, dh = q.shape
    hkv = k.shape[2]
    g = hq // hkv
    nblk = s // BLOCK
    qb = q.reshape(b, nblk, BLOCK, hkv, g, dh)

    def band(t):
        tb = t.reshape(b, nblk, BLOCK, hkv, dh)
        prev = jnp.pad(tb, ((0, 0), (1, 0), (0, 0), (0, 0), (0, 0)))[:, :-1]
        return jnp.concatenate([prev, tb], axis=2)

    kb, vb = band(k), band(v)
    z = jnp.einsum('bnqhgd,bnkhd->bnhgqk', qb, kb).astype(jnp.float32) / math.sqrt(dh)
    dist = jnp.arange(BLOCK)[:, None] - jnp.arange(2 * BLOCK)[None, :] + BLOCK
    in_window = (dist >= 0) & (dist < WINDOW)
    key_valid = (jnp.arange(nblk)[:, None] * BLOCK - BLOCK + jnp.arange(2 * BLOCK)[None, :]) >= 0
    mask = in_window[None, :, :] & key_valid[:, None, :]
    m_h = slopes.astype(jnp.float32).reshape(hkv, g)[:, :, None, None]
    z = z - m_h * dist.astype(jnp.float32)
    z = jnp.where(mask[None, :, None, None], z, NEG_INF)
    sink = sinks.astype(jnp.float32).reshape(hkv, g)[None, None, :, :, None, None]
    zmax = jnp.maximum(jnp.max(z, axis=-1, keepdims=True), sink)
    p = jnp.exp(z - zmax)
    p = p / (jnp.sum(p, axis=-1, keepdims=True) + jnp.exp(sink - zmax))
    o = jnp.einsum('bnhgqk,bnkhd->bnqhgd', p.astype(vb.dtype), vb)
    return o.reshape(b, s, hq * dh)


def conv_ffn(x, w_up, conv_w, conv_b, w_down):
    hdn = x @ w_up
    c = hdn.shape[-1]
    hc = lax.conv_general_dilated(
        hdn, conv_w.reshape(CONV_WIDTH, 1, c).astype(hdn.dtype),
        window_strides=(1,), padding=[(CONV_WIDTH - 1, 0)],
        dimension_numbers=('NWC', 'WIO', 'NWC'), feature_group_count=c) + conv_b
    gate, up = jnp.split(hc, 2, axis=-1)
    return (jax.nn.silu(gate) * up) @ w_down


def setup_inputs(seed: int = 0) -> dict:
    key = jax.random.key(seed)
    ks = jax.random.split(key, 14)
    f32 = jnp.float32
    nrm = lambda k, shape, fan: jax.random.normal(k, shape, f32) * (fan ** -0.5)
    return {
        "x": jax.random.normal(ks[0], (BATCH, SEQ, D_MODEL), f32),
        "norm_mix_g": 1.0 + 0.02 * jax.random.normal(ks[1], (DEPTH, D_MODEL), f32),
        "w_in": nrm(ks[2], (DEPTH, D_MODEL, IN_W), D_MODEL),
        "w_sb_out": nrm(ks[3], (DEPTH, SB_W, D_MODEL), SB_W),
        "w_swa_out": nrm(ks[4], (DEPTH, SWA_Q_W, D_MODEL), SWA_Q_W),
        "w_o": nrm(ks[5], (DEPTH, D_MODEL, D_MODEL), D_MODEL),
        "sinks": 0.5 * jax.random.normal(ks[6], (DEPTH, SWA_Q_HEADS), f32),
        "norm_ffn_g": 1.0 + 0.02 * jax.random.normal(ks[7], (DEPTH, D_MODEL), f32),
        "w_up": nrm(ks[8], (DEPTH, D_MODEL, 2 * D_FF), D_MODEL),
        "conv_w": nrm(ks[9], (DEPTH, CONV_WIDTH, 2 * D_FF), CONV_WIDTH),
        "conv_b": 0.01 * jax.random.normal(ks[10], (DEPTH, 2 * D_FF), f32),
        "w_down": nrm(ks[11], (DEPTH, D_FF, D_MODEL), D_FF),
        "norm_final_g": 1.0 + 0.02 * jax.random.normal(ks[12], (D_MODEL,), f32),
    }


def reference(x, norm_mix_g, w_in, w_sb_out, w_swa_out, w_o, sinks, norm_ffn_g,
              w_up, conv_w, conv_b, w_down, norm_final_g):
    b, s, _ = x.shape
    slopes = alibi_slopes(SWA_Q_HEADS)
    h = x
    for l in range(DEPTH):
        xn = rmsnorm(h, norm_mix_g[l])
        proj = xn @ w_in[l]
        sb_q, sb_k, sb_v, sw_q, sw_k, sw_v, gate_a, gate_b = jnp.split(proj, IN_SPLITS, axis=-1)
        heads = lambda t, n: t.reshape(b, s, n, HEAD_DIM)
        y_a = stick_breaking_attention(heads(sb_q, SB_HEADS), heads(sb_k, SB_HEADS),
                                       heads(sb_v, SB_HEADS)) @ w_sb_out[l]
        y_b = sliding_window_gqa(heads(sw_q, SWA_Q_HEADS), heads(sw_k, SWA_KV_HEADS),
                                 heads(sw_v, SWA_KV_HEADS), sinks[l], slopes) @ w_swa_out[l]
        mixed = jax.nn.sigmoid(gate_a) * y_a + jax.nn.sigmoid(gate_b) * y_b
        h = h + mixed @ w_o[l]
        h = h + conv_ffn(rmsnorm(h, norm_ffn_g[l]), w_up[l], conv_w[l], conv_b[l], w_down[l])
    return rmsnorm(h, norm_final_g)
```

```python
import functools
import math

import jax
import jax.numpy as jnp
import numpy as np
from jax import lax
from jax.experimental import pallas as pl
from jax.experimental.pallas import tpu as pltpu

D_MODEL = 2048
HEAD_DIM = 64
SB_HEADS = 16
SWA_Q_HEADS = 16
SWA_KV_HEADS = 2
SWA_GROUP = SWA_Q_HEADS // SWA_KV_HEADS
WINDOW = 128
D_FF = 5632
CONV_WIDTH = 3
RMS_EPS = 1e-5
NEG_INF = -1e30

SB_W = SB_HEADS * HEAD_DIM
SWA_Q_W = SWA_Q_HEADS * HEAD_DIM
SWA_KV_W = SWA_KV_HEADS * HEAD_DIM
QKV_W = 3 * SB_W + SWA_Q_W + 2 * SWA_KV_W
LANES = 128

BF16 = jnp.bfloat16
F32 = jnp.float32

VMEM_LIMIT = 56 * 1024 * 1024


def _params(*sem):
    return pltpu.CompilerParams(dimension_semantics=sem, vmem_limit_bytes=VMEM_LIMIT)


def _resident(shape, index_map):
    return pl.BlockSpec(shape, index_map, pipeline_mode=pl.Buffered(1))


def _rms(xf, g):
    return xf * lax.rsqrt(jnp.mean(xf * xf, axis=-1, keepdims=True) + RMS_EPS) * g


def _inproj_kernel(x_ref, g_ref, w_ref, xn_ref, qkv_ref):
    xn = _rms(x_ref[...], g_ref[...]).astype(BF16)
    xn_ref[...] = xn
    qkv_ref[...] = jnp.dot(xn, w_ref[...], preferred_element_type=F32).astype(BF16)


def _inproj(x2, g, w_qkv, tm=512):
    t = x2.shape[0]
    return pl.pallas_call(
        _inproj_kernel,
        grid=(t // tm,),
        in_specs=[
            pl.BlockSpec((tm, D_MODEL), lambda m: (m, 0)),
            _resident((1, D_MODEL), lambda m: (0, 0)),
            _resident((D_MODEL, QKV_W), lambda m: (0, 0)),
        ],
        out_specs=[
            pl.BlockSpec((tm, D_MODEL), lambda m: (m, 0)),
            pl.BlockSpec((tm, QKV_W), lambda m: (m, 0)),
        ],
        out_shape=[
            jax.ShapeDtypeStruct((t, D_MODEL), BF16),
            jax.ShapeDtypeStruct((t, QKV_W), BF16),
        ],
        compiler_params=_params("arbitrary"),
        name="inproj",
    )(x2, g, w_qkv)


def _sb_kernel(q_ref, k_ref, vt_ref, o_ref, *, blk):
    i = pl.program_id(2)
    q = q_ref[...] * BF16(1.0 / math.sqrt(HEAD_DIM))
    lane = lax.broadcasted_iota(jnp.int32, (blk, LANES), 1)
    row = lax.broadcasted_iota(jnp.int32, (blk, blk), 0)
    col = lax.broadcasted_iota(jnp.int32, (blk, blk), 1)
    later = (col > row).astype(BF16)
    causal = row < col

    def block(j, h, qh, carry, acc, masked):
        kb = k_ref[pl.ds(j * blk, blk), :]
        z = lax.dot_general(kb, qh, (((1,), (1,)), ((), ())), preferred_element_type=F32)
        l1p = jnp.log(1.0 + jnp.exp(-jnp.abs(z)))
        lom = -jnp.maximum(z, 0.0) - l1p
        lb = lom + z
        if masked:
            lom = jnp.where(causal, lom, 0.0)
        hi = lom.astype(BF16)
        lo = (lom - hi.astype(F32)).astype(BF16)
        suffix = (jnp.dot(later, hi, preferred_element_type=F32)
                  + jnp.dot(later, lo, preferred_element_type=F32))
        a = jnp.exp(lb + suffix + carry)
        if masked:
            a = jnp.where(causal, a, 0.0)
        vt = vt_ref[0, j, h * HEAD_DIM:(h + 1) * HEAD_DIM, :]
        acc = acc + jnp.dot(vt, a.astype(BF16), preferred_element_type=F32)
        carry = carry + jnp.sum(lom, axis=0, keepdims=True)
        return carry, acc

    accs = []
    for h in range(2):
        qh = jnp.where((lane >= HEAD_DIM) == (h == 1), q, jnp.zeros_like(q))
        carry0 = jnp.zeros((1, blk), F32)
        acc0 = jnp.zeros((HEAD_DIM, blk), F32)
        carry, acc = block(i, h, qh, carry0, acc0, True)

        def body(jj, state, h=h, qh=qh):
            return block(i - jj, h, qh, state[0], state[1], False)

        carry, acc = lax.fori_loop(1, i + 1, body, (carry, acc))
        accs.append(acc)
    o_ref[...] = jnp.concatenate(accs, axis=0).T.astype(BF16)


def _sb_attention(qkv, vt, batch, seq, blk=256):
    nq = seq // blk
    pairs = SB_W // LANES
    return pl.pallas_call(
        functools.partial(_sb_kernel, blk=blk),
        grid=(batch, pairs, nq),
        in_specs=[
            pl.BlockSpec((blk, LANES), lambda b, p, i: (b * nq + i, p)),
            pl.BlockSpec((seq, LANES), lambda b, p, i: (b, pairs + p)),
            pl.BlockSpec((1, nq, LANES, blk), lambda b, p, i: (b, 0, p, 0)),
        ],
        out_specs=pl.BlockSpec((blk, LANES), lambda b, p, i: (b * nq + i, p)),
        out_shape=jax.ShapeDtypeStruct((batch * seq, SB_W), BF16),
        compiler_params=_params("arbitrary", "arbitrary", "arbitrary"),
        name="sb_attn",
    )(qkv, qkv, vt)


def _swa_kernel(sink_ref, q_ref, kp_ref, kc_ref, vp_ref, vc_ref, o_ref, *, slopes):
    n = pl.program_id(1)
    blk = q_ref.shape[0]
    q = q_ref[...] * BF16(1.0 / math.sqrt(HEAD_DIM))
    r = lax.broadcasted_iota(jnp.int32, (blk, 2 * blk), 0)
    c = lax.broadcasted_iota(jnp.int32, (blk, 2 * blk), 1)
    dist = r - c + blk
    mask = (dist >= 0) & (dist < WINDOW) & ((c >= blk) | (n > 0))
    distf = dist.astype(F32)
    kb = jnp.concatenate([kp_ref[...], kc_ref[...]], axis=0)
    vb = jnp.concatenate([vp_ref[...], vc_ref[...]], axis=0)
    outs = []
    for h in range(SWA_Q_HEADS):
        g = h // SWA_GROUP
        qh = q[:, h * HEAD_DIM:(h + 1) * HEAD_DIM]
        kg = kb[:, g * HEAD_DIM:(g + 1) * HEAD_DIM]
        vg = vb[:, g * HEAD_DIM:(g + 1) * HEAD_DIM]
        z = lax.dot_general(qh, kg, (((1,), (1,)), ((), ())), preferred_element_type=F32)
        z = jnp.where(mask, z - slopes[h] * distf, NEG_INF)
        sink = sink_ref[h]
        zmax = jnp.maximum(jnp.max(z, axis=-1, keepdims=True), sink)
        p = jnp.exp(z - zmax)
        denom = jnp.sum(p, axis=-1, keepdims=True) + jnp.exp(sink - zmax)
        o = jnp.dot(p.astype(BF16), vg, preferred_element_type=F32)
        outs.append(o / denom)
    o_ref[...] = jnp.concatenate(outs, axis=1).astype(BF16)


def _swa_attention(qkv, sinks, batch, seq, blk=128):
    nb = seq // blk
    qcol = 3 * SB_W // SWA_Q_W
    kcol = (3 * SB_W + SWA_Q_W) // SWA_KV_W
    vcol = kcol + 1
    slopes = [float(v) for v in np.power(2.0, -8.0 * np.arange(1, SWA_Q_HEADS + 1) / SWA_Q_HEADS).astype(np.float32)]
    cur = lambda col: (lambda b, n: (b * nb + n, col))
    prev = lambda col: (lambda b, n: (b * nb + jnp.maximum(n - 1, 0), col))
    return pl.pallas_call(
        functools.partial(_swa_kernel, slopes=slopes),
        grid=(batch, nb),
        in_specs=[
            pl.BlockSpec(memory_space=pltpu.SMEM),
            pl.BlockSpec((blk, SWA_Q_W), cur(qcol)),
            pl.BlockSpec((blk, SWA_KV_W), prev(kcol)),
            pl.BlockSpec((blk, SWA_KV_W), cur(kcol)),
            pl.BlockSpec((blk, SWA_KV_W), prev(vcol)),
            pl.BlockSpec((blk, SWA_KV_W), cur(vcol)),
        ],
        out_specs=pl.BlockSpec((blk, SWA_Q_W), lambda b, n: (b * nb + n, 0)),
        out_shape=jax.ShapeDtypeStruct((batch * seq, SWA_Q_W), BF16),
        compiler_params=_params("arbitrary", "arbitrary"),
        name="swa_attn",
    )(sinks, qkv, qkv, qkv, qkv, qkv)


def _mix_kernel(xn_ref, wga_ref, wgb_ref, a_ref, wa_ref, b_ref, wb_ref, o_ref):
    xn = xn_ref[...]
    ga = jnp.dot(xn, wga_ref[...], preferred_element_type=F32)
    gb = jnp.dot(xn, wgb_ref[...], preferred_element_type=F32)
    ya = jnp.dot(a_ref[...], wa_ref[...], preferred_element_type=F32)
    yb = jnp.dot(b_ref[...], wb_ref[...], preferred_element_type=F32)
    o_ref[...] = (jax.nn.sigmoid(ga) * ya + jax.nn.sigmoid(gb) * yb).astype(BF16)


def _mix(xn, w_ga, w_gb, att_a, w_a, att_b, w_b, tm=1024, tn=512):
    t = xn.shape[0]
    return pl.pallas_call(
        _mix_kernel,
        grid=(t // tm, D_MODEL // tn),
        in_specs=[
            pl.BlockSpec((tm, D_MODEL), lambda m, n: (m, 0)),
            pl.BlockSpec((D_MODEL, tn), lambda m, n: (0, n)),
            pl.BlockSpec((D_MODEL, tn), lambda m, n: (0, n)),
            pl.BlockSpec((tm, SB_W), lambda m, n: (m, 0)),
            pl.BlockSpec((SB_W, tn), lambda m, n: (0, n)),
            pl.BlockSpec((tm, SWA_Q_W), lambda m, n: (m, 0)),
            pl.BlockSpec((SWA_Q_W, tn), lambda m, n: (0, n)),
        ],
        out_specs=pl.BlockSpec((tm, tn), lambda m, n: (m, n)),
        out_shape=jax.ShapeDtypeStruct((t, D_MODEL), BF16),
        compiler_params=_params("arbitrary", "arbitrary"),
        name="mix",
    )(xn, w_ga, w_gb, att_a, w_a, att_b, w_b)


def _oproj_kernel(x_ref, mixed_ref, w_ref, g_ref, h_ref, xn_ref):
    h = x_ref[...] + jnp.dot(mixed_ref[...], w_ref[...], preferred_element_type=F32)
    h_ref[...] = h
    xn_ref[...] = _rms(h, g_ref[...]).astype(BF16)


def _oproj(x2, mixed, w_o, g, tm=512):
    t = x2.shape[0]
    return pl.pallas_call(
        _oproj_kernel,
        grid=(t // tm,),
        in_specs=[
            pl.BlockSpec((tm, D_MODEL), lambda m: (m, 0)),
            pl.BlockSpec((tm, D_MODEL), lambda m: (m, 0)),
            _resident((D_MODEL, D_MODEL), lambda m: (0, 0)),
            _resident((1, D_MODEL), lambda m: (0, 0)),
        ],
        out_specs=[
            pl.BlockSpec((tm, D_MODEL), lambda m: (m, 0)),
            pl.BlockSpec((tm, D_MODEL), lambda m: (m, 0)),
        ],
        out_shape=[
            jax.ShapeDtypeStruct((t, D_MODEL), F32),
            jax.ShapeDtypeStruct((t, D_MODEL), BF16),
        ],
        compiler_params=_params("arbitrary"),
        name="oproj",
    )(x2, mixed, w_o, g)


CONV_HALO = 8


def _ffn_up_kernel(xn_ref, wg_ref, wu_ref, cg_ref, cu_ref, bg_ref, bu_ref, o_ref, hg_ref, hu_ref, *, tiles_per_seq):
    m = pl.program_id(1)
    tm = xn_ref.shape[0]
    xn = xn_ref[...]
    for h_ref, w_ref in ((hg_ref, wg_ref), (hu_ref, wu_ref)):
        @pl.when(m % tiles_per_seq == 0)
        def _():
            h_ref[0:CONV_HALO, :] = jnp.zeros((CONV_HALO, h_ref.shape[1]), F32)

        @pl.when(m % tiles_per_seq != 0)
        def _():
            h_ref[0:CONV_HALO, :] = h_ref[tm:tm + CONV_HALO, :]

        h_ref[CONV_HALO:CONV_HALO + tm, :] = jnp.dot(xn, w_ref[...], preferred_element_type=F32)

    def conv(h_ref, c_ref, b_ref):
        out = b_ref[...]
        for k in range(CONV_WIDTH):
            start = CONV_HALO - (CONV_WIDTH - 1) + k
            out = out + c_ref[k:k + 1, :] * h_ref[start:start + tm, :]
        return out

    gate = conv(hg_ref, cg_ref, bg_ref)
    up = conv(hu_ref, cu_ref, bu_ref)
    o_ref[...] = (gate * jax.nn.sigmoid(gate) * up).astype(BF16)


def _ffn_up(xn2, w_up, conv_w, conv_b, seq, tm=1024, tn=512):
    t = xn2.shape[0]
    nn = D_FF // tn
    kern = functools.partial(_ffn_up_kernel, tiles_per_seq=seq // tm)
    return pl.pallas_call(
        kern,
        grid=(nn, t // tm),
        in_specs=[
            pl.BlockSpec((tm, D_MODEL), lambda n, m: (m, 0)),
            pl.BlockSpec((D_MODEL, tn), lambda n, m: (0, n)),
            pl.BlockSpec((D_MODEL, tn), lambda n, m: (0, nn + n)),
            pl.BlockSpec((CONV_WIDTH, tn), lambda n, m: (0, n)),
            pl.BlockSpec((CONV_WIDTH, tn), lambda n, m: (0, nn + n)),
            pl.BlockSpec((1, tn), lambda n, m: (0, n)),
            pl.BlockSpec((1, tn), lambda n, m: (0, nn + n)),
        ],
        out_specs=pl.BlockSpec((tm, tn), lambda n, m: (m, n)),
        out_shape=jax.ShapeDtypeStruct((t, D_FF), BF16),
        scratch_shapes=[pltpu.VMEM((CONV_HALO + tm, tn), F32), pltpu.VMEM((CONV_HALO + tm, tn), F32)],
        compiler_params=_params("arbitrary", "arbitrary"),
        name="ffn_up",
    )(xn2, w_up, w_up, conv_w, conv_w, conv_b, conv_b)


def _ffn_down_kernel(h_ref, act_ref, w_ref, g_ref, o_ref):
    h = h_ref[...] + jnp.dot(act_ref[...], w_ref[...], preferred_element_type=F32)
    o_ref[...] = _rms(h, g_ref[...])


def _ffn_down(h1, act, w_down, g, tm=256):
    t = h1.shape[0]
    return pl.pallas_call(
        _ffn_down_kernel,
        grid=(t // tm,),
        in_specs=[
            pl.BlockSpec((tm, D_MODEL), lambda m: (m, 0)),
            pl.BlockSpec((tm, D_FF), lambda m: (m, 0)),
            _resident((D_FF, D_MODEL), lambda m: (0, 0)),
            _resident((1, D_MODEL), lambda m: (0, 0)),
        ],
        out_specs=pl.BlockSpec((tm, D_MODEL), lambda m: (m, 0)),
        out_shape=jax.ShapeDtypeStruct((t, D_MODEL), F32),
        compiler_params=_params("arbitrary"),
        name="ffn_down",
    )(h1, act, w_down, g)


def kernel(x, norm_mix_g, w_in, w_sb_out, w_swa_out, w_o, sinks, norm_ffn_g, w_up, conv_w, conv_b, w_down, norm_final_g):
    batch, seq, _ = x.shape
    assert w_in.shape[0] == 1, "single-layer block: the final norm is fused into the FFN down projection"
    sb_blk = 256
    row = lambda v: v.reshape(1, -1)
    x2 = x.reshape(batch * seq, D_MODEL)
    w_qkv = w_in[0, :, :QKV_W].astype(BF16)
    w_ga = w_in[0, :, QKV_W:QKV_W + D_MODEL].astype(BF16)
    w_gb = w_in[0, :, QKV_W + D_MODEL:].astype(BF16)
    xn, qkv = _inproj(x2, row(norm_mix_g[0]), w_qkv)
    v_sb = qkv[:, 2 * SB_W:3 * SB_W].reshape(batch, seq // sb_blk, sb_blk, SB_W)
    vt = jnp.transpose(v_sb, (0, 1, 3, 2))
    att_a = _sb_attention(qkv, vt, batch, seq, blk=sb_blk)
    att_b = _swa_attention(qkv, sinks[0], batch, seq)
    mixed = _mix(xn, w_ga, w_gb, att_a, w_sb_out[0].astype(BF16), att_b, w_swa_out[0].astype(BF16))
    h1, xn2 = _oproj(x2, mixed, w_o[0].astype(BF16), row(norm_ffn_g[0]))
    act = _ffn_up(xn2, w_up[0].astype(BF16), conv_w[0], row(conv_b[0]), seq)
    out = _ffn_down(h1, act, w_down[0].astype(BF16), row(norm_final_g))
    return out.reshape(batch, seq, D_MODEL)
```

```python
import functools
import math

import jax
import jax.numpy as jnp
import numpy as np
from jax import lax
from jax.experimental import pallas as pl
from jax.experimental.pallas import tpu as pltpu

D_MODEL = 2048
HEAD_DIM = 64
SB_HEADS = 16
SWA_Q_HEADS = 16
SWA_KV_HEADS = 2
SWA_GROUP = SWA_Q_HEADS // SWA_KV_HEADS
WINDOW = 128
D_FF = 5632
CONV_WIDTH = 3
RMS_EPS = 1e-5
NEG_INF = -1e30

SB_W = SB_HEADS * HEAD_DIM
SWA_Q_W = SWA_Q_HEADS * HEAD_DIM
SWA_KV_W = SWA_KV_HEADS * HEAD_DIM
QKV_W = 3 * SB_W + SWA_Q_W + 2 * SWA_KV_W
LANES = 128
SUBLANES = 8

BF16 = jnp.bfloat16
F32 = jnp.float32

VMEM_LIMIT = 56 * 1024 * 1024


def _params(*sem):
    return pltpu.CompilerParams(dimension_semantics=sem, vmem_limit_bytes=VMEM_LIMIT)


def _resident(shape, index_map):
    return pl.BlockSpec(shape, index_map, pipeline_mode=pl.Buffered(1))


def _rms(xf, g):
    return xf * lax.rsqrt(jnp.mean(xf * xf, axis=-1, keepdims=True) + RMS_EPS) * g


def _inproj_kernel(x_ref, g_ref, w_ref, xn_ref, qkv_ref):
    xn = _rms(x_ref[...], g_ref[...]).astype(BF16)
    xn_ref[...] = xn
    qkv_ref[...] = jnp.dot(xn, w_ref[...], preferred_element_type=F32).astype(BF16)


def _inproj(x2, g, w_qkv, tm=512):
    t = x2.shape[0]
    return pl.pallas_call(
        _inproj_kernel,
        grid=(t // tm,),
        in_specs=[
            pl.BlockSpec((tm, D_MODEL), lambda m: (m, 0)),
            _resident((1, D_MODEL), lambda m: (0, 0)),
            _resident((D_MODEL, QKV_W), lambda m: (0, 0)),
        ],
        out_specs=[
            pl.BlockSpec((tm, D_MODEL), lambda m: (m, 0)),
            pl.BlockSpec((tm, QKV_W), lambda m: (m, 0)),
        ],
        out_shape=[
            jax.ShapeDtypeStruct((t, D_MODEL), BF16),
            jax.ShapeDtypeStruct((t, QKV_W), BF16),
        ],
        compiler_params=_params("arbitrary"),
        name="inproj",
    )(x2, g, w_qkv)


SB_GROUP = 8
SB_CHUNK = 256
SB_SUB = 128
HI_MASK = 0xFFFF0000


def _sb_kernel(q_ref, k_ref, vt_ref, o_ref, *, blk):
    i = pl.program_id(2)
    width = SB_GROUP * HEAD_DIM
    n = SB_GROUP * blk
    nsub = blk // SB_SUB
    q = q_ref[...] * BF16(1.0 / math.sqrt(HEAD_DIM))
    qt = q.astype(F32).T.astype(BF16)
    chunk_heads = SB_CHUNK // HEAD_DIM
    head_of_row = lax.broadcasted_iota(jnp.int32, (SB_CHUNK, blk), 0) // HEAD_DIM
    qstacks = []
    for cc in range(width // SB_CHUNK):
        qt_c = qt[cc * SB_CHUNK:(cc + 1) * SB_CHUNK]
        qstacks.append(jnp.concatenate(
            [jnp.where(head_of_row == c, qt_c, jnp.zeros_like(qt_c)) for c in range(chunk_heads)], axis=1))
    r = lax.broadcasted_iota(jnp.int32, (SB_SUB, SB_SUB), 0)
    c = lax.broadcasted_iota(jnp.int32, (SB_SUB, SB_SUB), 1)
    incl = (c >= r).astype(BF16)
    incl2 = jnp.concatenate([incl, incl], axis=1)
    key_idx = lax.broadcasted_iota(jnp.int32, (blk, n), 0)
    query_idx = lax.broadcasted_iota(jnp.int32, (blk, n), 1) & (blk - 1)
    causal = key_idx < query_idx
    last_sublane = lax.broadcasted_iota(jnp.int32, (SUBLANES, n), 0) == SUBLANES - 1

    def scores(j):
        kb = k_ref[pl.ds(j * blk, blk), :]
        zs = [jnp.dot(kb[:, cc * SB_CHUNK:(cc + 1) * SB_CHUNK], qs, preferred_element_type=F32)
              for cc, qs in enumerate(qstacks)]
        return jnp.concatenate(zs, axis=1)

    def weights(z, carry, masked):
        sp = jnp.where(z > 17.0, z, jnp.log(1.0 + jnp.exp(z)))
        if masked:
            sp = jnp.where(causal, sp, 0.0)
        ts = [None] * nsub
        for u in reversed(range(nsub)):
            lo_row, hi_row = u * SB_SUB, (u + 1) * SB_SUB
            tail = sp[hi_row - SUBLANES:hi_row] + jnp.where(last_sublane, carry, 0.0)
            spu = jnp.concatenate([sp[lo_row:hi_row - SUBLANES], tail], axis=0)
            hi = lax.bitcast_convert_type(lax.bitcast_convert_type(spu, jnp.uint32) & jnp.uint32(HI_MASK), F32)
            operand = jnp.concatenate([hi.astype(BF16), (spu - hi).astype(BF16)], axis=0)
            inc = jnp.dot(incl2, operand, preferred_element_type=F32)
            ts[u] = z[lo_row:hi_row] - inc
            if u > 0:
                carry = carry + jnp.sum(sp[lo_row:hi_row], axis=0, keepdims=True)
            else:
                carry = inc[0:1, :]
        a = jnp.exp(jnp.concatenate(ts, axis=0).astype(BF16))
        if masked:
            a = jnp.where(causal, a, jnp.zeros_like(a))
        return a, carry

    def values(j, a, accs):
        out = []
        for h in range(SB_GROUP):
            vt = vt_ref[0, j, h * HEAD_DIM:(h + 1) * HEAD_DIM, :]
            out.append(accs[h] + jnp.dot(vt, a[:, h * blk:(h + 1) * blk], preferred_element_type=F32))
        return tuple(out)

    def block(j, carry, accs, masked):
        a, carry = weights(scores(j), carry, masked)
        return carry, values(j, a, accs)

    carry0 = jnp.zeros((1, n), F32)
    accs0 = tuple(jnp.zeros((HEAD_DIM, blk), F32) for _ in range(SB_GROUP))
    state = block(i, carry0, accs0, True)
    _, accs = lax.fori_loop(1, i + 1, lambda jj, st: block(i - jj, st[0], st[1], False), state)
    o_ref[...] = jnp.concatenate(accs, axis=0).T.astype(BF16)


def _sb_attention(qkv, vt, batch, seq, blk=256):
    nq = seq // blk
    width = SB_GROUP * HEAD_DIM
    groups = SB_W // width
    return pl.pallas_call(
        functools.partial(_sb_kernel, blk=blk),
        grid=(batch, groups, nq),
        in_specs=[
            pl.BlockSpec((blk, width), lambda b, g, i: (b * nq + i, g)),
            pl.BlockSpec((seq, width), lambda b, g, i: (b, groups + g)),
            pl.BlockSpec((1, nq, width, blk), lambda b, g, i: (b, 0, g, 0)),
        ],
        out_specs=pl.BlockSpec((blk, width), lambda b, g, i: (b * nq + i, g)),
        out_shape=jax.ShapeDtypeStruct((batch * seq, SB_W), BF16),
        compiler_params=_params("arbitrary", "arbitrary", "arbitrary"),
        name="sb_attn",
    )(qkv, qkv, vt)


def _swa_kernel(sink_ref, q_ref, kp_ref, kc_ref, vp_ref, vc_ref, o_ref, *, slopes):
    n = pl.program_id(1)
    blk = q_ref.shape[0]
    q = q_ref[...] * BF16(1.0 / math.sqrt(HEAD_DIM))
    r = lax.broadcasted_iota(jnp.int32, (blk, 2 * blk), 0)
    c = lax.broadcasted_iota(jnp.int32, (blk, 2 * blk), 1)
    dist = r - c + blk
    mask = (dist >= 0) & (dist < WINDOW) & ((c >= blk) | (n > 0))
    distf = dist.astype(F32)
    kb = jnp.concatenate([kp_ref[...], kc_ref[...]], axis=0)
    vb = jnp.concatenate([vp_ref[...], vc_ref[...]], axis=0)
    outs = []
    for h in range(SWA_Q_HEADS):
        g = h // SWA_GROUP
        qh = q[:, h * HEAD_DIM:(h + 1) * HEAD_DIM]
        kg = kb[:, g * HEAD_DIM:(g + 1) * HEAD_DIM]
        vg = vb[:, g * HEAD_DIM:(g + 1) * HEAD_DIM]
        z = lax.dot_general(qh, kg, (((1,), (1,)), ((), ())), preferred_element_type=F32)
        z = jnp.where(mask, z - slopes[h] * distf, NEG_INF)
        sink = sink_ref[h]
        zmax = jnp.maximum(jnp.max(z, axis=-1, keepdims=True), sink)
        p = jnp.exp(z - zmax)
        denom = jnp.sum(p, axis=-1, keepdims=True) + jnp.exp(sink - zmax)
        o = jnp.dot(p.astype(BF16), vg, preferred_element_type=F32)
        outs.append(o / denom)
    o_ref[...] = jnp.concatenate(outs, axis=1).astype(BF16)


def _swa_attention(qkv, sinks, batch, seq, blk=128):
    nb = seq // blk
    qcol = 3 * SB_W // SWA_Q_W
    kcol = (3 * SB_W + SWA_Q_W) // SWA_KV_W
    vcol = kcol + 1
    slopes = [float(v) for v in np.power(2.0, -8.0 * np.arange(1, SWA_Q_HEADS + 1) / SWA_Q_HEADS).astype(np.float32)]
    cur = lambda col: (lambda b, n: (b * nb + n, col))
    prev = lambda col: (lambda b, n: (b * nb + jnp.maximum(n - 1, 0), col))
    return pl.pallas_call(
        functools.partial(_swa_kernel, slopes=slopes),
        grid=(batch, nb),
        in_specs=[
            pl.BlockSpec(memory_space=pltpu.SMEM),
            pl.BlockSpec((blk, SWA_Q_W), cur(qcol)),
            pl.BlockSpec((blk, SWA_KV_W), prev(kcol)),
            pl.BlockSpec((blk, SWA_KV_W), cur(kcol)),
            pl.BlockSpec((blk, SWA_KV_W), prev(vcol)),
            pl.BlockSpec((blk, SWA_KV_W), cur(vcol)),
        ],
        out_specs=pl.BlockSpec((blk, SWA_Q_W), lambda b, n: (b * nb + n, 0)),
        out_shape=jax.ShapeDtypeStruct((batch * seq, SWA_Q_W), BF16),
        compiler_params=_params("arbitrary", "arbitrary"),
        name="swa_attn",
    )(sinks, qkv, qkv, qkv, qkv, qkv)


def _mix_kernel(xn_ref, wga_ref, wgb_ref, a_ref, wa_ref, b_ref, wb_ref, o_ref):
    xn = xn_ref[...]
    ga = jnp.dot(xn, wga_ref[...], preferred_element_type=F32)
    gb = jnp.dot(xn, wgb_ref[...], preferred_element_type=F32)
    ya = jnp.dot(a_ref[...], wa_ref[...], preferred_element_type=F32)
    yb = jnp.dot(b_ref[...], wb_ref[...], preferred_element_type=F32)
    o_ref[...] = (jax.nn.sigmoid(ga) * ya + jax.nn.sigmoid(gb) * yb).astype(BF16)


def _mix(xn, w_ga, w_gb, att_a, w_a, att_b, w_b, tm=1024, tn=512):
    t = xn.shape[0]
    return pl.pallas_call(
        _mix_kernel,
        grid=(t // tm, D_MODEL // tn),
        in_specs=[
            pl.BlockSpec((tm, D_MODEL), lambda m, n: (m, 0)),
            pl.BlockSpec((D_MODEL, tn), lambda m, n: (0, n)),
            pl.BlockSpec((D_MODEL, tn), lambda m, n: (0, n)),
            pl.BlockSpec((tm, SB_W), lambda m, n: (m, 0)),
            pl.BlockSpec((SB_W, tn), lambda m, n: (0, n)),
            pl.BlockSpec((tm, SWA_Q_W), lambda m, n: (m, 0)),
            pl.BlockSpec((SWA_Q_W, tn), lambda m, n: (0, n)),
        ],
        out_specs=pl.BlockSpec((tm, tn), lambda m, n: (m, n)),
        out_shape=jax.ShapeDtypeStruct((t, D_MODEL), BF16),
        compiler_params=_params("arbitrary", "arbitrary"),
        name="mix",
    )(xn, w_ga, w_gb, att_a, w_a, att_b, w_b)


def _oproj_kernel(x_ref, mixed_ref, w_ref, g_ref, h_ref, xn_ref):
    h = x_ref[...] + jnp.dot(mixed_ref[...], w_ref[...], preferred_element_type=F32)
    h_ref[...] = h
    xn_ref[...] = _rms(h, g_ref[...]).astype(BF16)


def _oproj(x2, mixed, w_o, g, tm=512):
    t = x2.shape[0]
    return pl.pallas_call(
        _oproj_kernel,
        grid=(t // tm,),
        in_specs=[
            pl.BlockSpec((tm, D_MODEL), lambda m: (m, 0)),
            pl.BlockSpec((tm, D_MODEL), lambda m: (m, 0)),
            _resident((D_MODEL, D_MODEL), lambda m: (0, 0)),
            _resident((1, D_MODEL), lambda m: (0, 0)),
        ],
        out_specs=[
            pl.BlockSpec((tm, D_MODEL), lambda m: (m, 0)),
            pl.BlockSpec((tm, D_MODEL), lambda m: (m, 0)),
        ],
        out_shape=[
            jax.ShapeDtypeStruct((t, D_MODEL), F32),
            jax.ShapeDtypeStruct((t, D_MODEL), BF16),
        ],
        compiler_params=_params("arbitrary"),
        name="oproj",
    )(x2, mixed, w_o, g)


CONV_HALO = 8
FFN_ROWS = 256


def _ffn_up_kernel(xn_ref, wg_ref, wu_ref, cg_ref, cu_ref, bg_ref, bu_ref, o_ref, hg_ref, hu_ref, *, tiles_per_seq):
    m = pl.program_id(1)
    tm = xn_ref.shape[0]
    for h_ref in (hg_ref, hu_ref):
        @pl.when(m % tiles_per_seq == 0)
        def _():
            h_ref[0:CONV_HALO, :] = jnp.zeros((CONV_HALO, h_ref.shape[1]), F32)

        @pl.when(m % tiles_per_seq != 0)
        def _():
            h_ref[0:CONV_HALO, :] = h_ref[tm:tm + CONV_HALO, :]

    def conv(h_ref, c_ref, b_ref, r0):
        out = b_ref[...]
        for k in range(CONV_WIDTH):
            start = r0 + CONV_HALO - (CONV_WIDTH - 1) + k
            out = out + c_ref[k:k + 1, :] * h_ref[start:start + FFN_ROWS, :]
        return out

    for r0 in range(0, tm, FFN_ROWS):
        xn = xn_ref[r0:r0 + FFN_ROWS, :]
        for h_ref, w_ref in ((hg_ref, wg_ref), (hu_ref, wu_ref)):
            h_ref[CONV_HALO + r0:CONV_HALO + r0 + FFN_ROWS, :] = jnp.dot(
                xn, w_ref[...], preferred_element_type=F32)
        gate = conv(hg_ref, cg_ref, bg_ref, r0)
        up = conv(hu_ref, cu_ref, bu_ref, r0)
        o_ref[r0:r0 + FFN_ROWS, :] = (gate * jax.nn.sigmoid(gate) * up).astype(BF16)


def _ffn_up(xn2, w_up, conv_w, conv_b, seq, tm=1024, tn=512):
    t = xn2.shape[0]
    nn = D_FF // tn
    kern = functools.partial(_ffn_up_kernel, tiles_per_seq=seq // tm)
    return pl.pallas_call(
        kern,
        grid=(nn, t // tm),
        in_specs=[
            pl.BlockSpec((tm, D_MODEL), lambda n, m: (m, 0)),
            pl.BlockSpec((D_MODEL, tn), lambda n, m: (0, n)),
            pl.BlockSpec((D_MODEL, tn), lambda n, m: (0, nn + n)),
            pl.BlockSpec((CONV_WIDTH, tn), lambda n, m: (0, n)),
            pl.BlockSpec((CONV_WIDTH, tn), lambda n, m: (0, nn + n)),
            pl.BlockSpec((1, tn), lambda n, m: (0, n)),
            pl.BlockSpec((1, tn), lambda n, m: (0, nn + n)),
        ],
        out_specs=pl.BlockSpec((tm, tn), lambda n, m: (m, n)),
        out_shape=jax.ShapeDtypeStruct((t, D_FF), BF16),
        scratch_shapes=[pltpu.VMEM((CONV_HALO + tm, tn), F32), pltpu.VMEM((CONV_HALO + tm, tn), F32)],
        compiler_params=_params("arbitrary", "arbitrary"),
        name="ffn_up",
    )(xn2, w_up, w_up, conv_w, conv_w, conv_b, conv_b)


def _ffn_down_kernel(h_ref, act_ref, w_ref, g_ref, o_ref):
    h = h_ref[...] + jnp.dot(act_ref[...], w_ref[...], preferred_element_type=F32)
    o_ref[...] = _rms(h, g_ref[...])


def _ffn_down(h1, act, w_down, g, tm=256):
    t = h1.shape[0]
    return pl.pallas_call(
        _ffn_down_kernel,
        grid=(t // tm,),
        in_specs=[
            pl.BlockSpec((tm, D_MODEL), lambda m: (m, 0)),
            pl.BlockSpec((tm, D_FF), lambda m: (m, 0)),
            _resident((D_FF, D_MODEL), lambda m: (0, 0)),
            _resident((1, D_MODEL), lambda m: (0, 0)),
        ],
        out_specs=pl.BlockSpec((tm, D_MODEL), lambda m: (m, 0)),
        out_shape=jax.ShapeDtypeStruct((t, D_MODEL), F32),
        compiler_params=_params("arbitrary"),
        name="ffn_down",
    )(h1, act, w_down, g)


def kernel(x, norm_mix_g, w_in, w_sb_out, w_swa_out, w_o, sinks, norm_ffn_g, w_up, conv_w, conv_b, w_down, norm_final_g):
    batch, seq, _ = x.shape
    assert w_in.shape[0] == 1, "single-layer block: the final norm is fused into the FFN down projection"
    sb_blk = 256
    row = lambda v: v.reshape(1, -1)
    x2 = x.reshape(batch * seq, D_MODEL)
    w_qkv = w_in[0, :, :QKV_W].astype(BF16)
    w_ga = w_in[0, :, QKV_W:QKV_W + D_MODEL].astype(BF16)
    w_gb = w_in[0, :, QKV_W + D_MODEL:].astype(BF16)
    xn, qkv = _inproj(x2, row(norm_mix_g[0]), w_qkv)
    v_sb = qkv[:, 2 * SB_W:3 * SB_W].reshape(batch, seq // sb_blk, sb_blk, SB_W)
    vt = jnp.transpose(v_sb, (0, 1, 3, 2))
    att_a = _sb_attention(qkv, vt, batch, seq, blk=sb_blk)
    att_b = _swa_attention(qkv, sinks[0], batch, seq)
    mixed = _mix(xn, w_ga, w_gb, att_a, w_sb_out[0].astype(BF16), att_b, w_swa_out[0].astype(BF16))
    h1, xn2 = _oproj(x2, mixed, w_o[0].astype(BF16), row(norm_ffn_g[0]))
    act = _ffn_up(xn2, w_up[0].astype(BF16), conv_w[0], row(conv_b[0]), seq)
    out = _ffn_down(h1, act, w_down[0].astype(BF16), row(norm_final_g))
    return out.reshape(batch, seq, D_MODEL)
```

```python
import functools
import math

import jax
import jax.numpy as jnp
import numpy as np
from jax import lax
from jax.experimental import pallas as pl
from jax.experimental.pallas import tpu as pltpu

D_MODEL = 2048
HEAD_DIM = 64
SB_HEADS = 16
SWA_Q_HEADS = 16
SWA_KV_HEADS = 2
SWA_GROUP = SWA_Q_HEADS // SWA_KV_HEADS
WINDOW = 128
D_FF = 5632
CONV_WIDTH = 3
RMS_EPS = 1e-5
NEG_INF = -1e30

SB_W = SB_HEADS * HEAD_DIM
SWA_Q_W = SWA_Q_HEADS * HEAD_DIM
SWA_KV_W = SWA_KV_HEADS * HEAD_DIM
QKV_W = 3 * SB_W + SWA_Q_W + 2 * SWA_KV_W
LANES = 128
SUBLANES = 8

BF16 = jnp.bfloat16
F32 = jnp.float32

VMEM_LIMIT = 56 * 1024 * 1024


def _params(*sem):
    return pltpu.CompilerParams(dimension_semantics=sem, vmem_limit_bytes=VMEM_LIMIT)


def _resident(shape, index_map):
    return pl.BlockSpec(shape, index_map, pipeline_mode=pl.Buffered(1))


def _rms(xf, g):
    return xf * lax.rsqrt(jnp.mean(xf * xf, axis=-1, keepdims=True) + RMS_EPS) * g


def _inproj_kernel(x_ref, g_ref, w_ref, xn_ref, qkv_ref):
    xn = _rms(x_ref[...], g_ref[...]).astype(BF16)
    xn_ref[...] = xn
    qkv_ref[...] = jnp.dot(xn, w_ref[...], preferred_element_type=F32).astype(BF16)


def _inproj(x2, g, w_qkv, tm=512):
    t = x2.shape[0]
    return pl.pallas_call(
        _inproj_kernel,
        grid=(t // tm,),
        in_specs=[
            pl.BlockSpec((tm, D_MODEL), lambda m: (m, 0)),
            _resident((1, D_MODEL), lambda m: (0, 0)),
            _resident((D_MODEL, QKV_W), lambda m: (0, 0)),
        ],
        out_specs=[
            pl.BlockSpec((tm, D_MODEL), lambda m: (m, 0)),
            pl.BlockSpec((tm, QKV_W), lambda m: (m, 0)),
        ],
        out_shape=[
            jax.ShapeDtypeStruct((t, D_MODEL), BF16),
            jax.ShapeDtypeStruct((t, QKV_W), BF16),
        ],
        compiler_params=_params("arbitrary"),
        name="inproj",
    )(x2, g, w_qkv)


SB_GROUP = 8
SB_CHUNK = 256
SB_SUB = 128
HI_MASK = 0xFFFF0000


def _sb_kernel(q_ref, k_ref, vt_ref, o_ref, za_ref, zb_ref, wa_ref, wb_ref, carry_ref, acc_ref, *, blk):
    i = pl.program_id(2)
    width = SB_GROUP * HEAD_DIM
    n = SB_GROUP * blk
    nsub = blk // SB_SUB
    q = q_ref[...] * BF16(1.0 / math.sqrt(HEAD_DIM))
    qt = q.astype(F32).T.astype(BF16)
    chunk_heads = SB_CHUNK // HEAD_DIM
    head_of_row = lax.broadcasted_iota(jnp.int32, (SB_CHUNK, blk), 0) // HEAD_DIM
    qstacks = []
    for cc in range(width // SB_CHUNK):
        qt_c = qt[cc * SB_CHUNK:(cc + 1) * SB_CHUNK]
        qstacks.append(jnp.concatenate(
            [jnp.where(head_of_row == c, qt_c, jnp.zeros_like(qt_c)) for c in range(chunk_heads)], axis=1))
    r = lax.broadcasted_iota(jnp.int32, (SB_SUB, SB_SUB), 0)
    c = lax.broadcasted_iota(jnp.int32, (SB_SUB, SB_SUB), 1)
    incl = (c >= r).astype(BF16)
    incl2 = jnp.concatenate([incl, incl], axis=1)
    key_idx = lax.broadcasted_iota(jnp.int32, (blk, n), 0)
    query_idx = lax.broadcasted_iota(jnp.int32, (blk, n), 1) & (blk - 1)
    causal = key_idx < query_idx
    last_sublane = lax.broadcasted_iota(jnp.int32, (SUBLANES, n), 0) == SUBLANES - 1

    def scores(j, z_ref):
        kb = k_ref[pl.ds(j * blk, blk), :]
        for cc, qs in enumerate(qstacks):
            z_ref[:, cc * chunk_heads * blk:(cc + 1) * chunk_heads * blk] = jnp.dot(
                kb[:, cc * SB_CHUNK:(cc + 1) * SB_CHUNK], qs, preferred_element_type=F32)

    def weights(z_ref, a_ref, masked):
        z = z_ref[...]
        carry = carry_ref[...]
        sp = jnp.where(z > 17.0, z, jnp.log(1.0 + jnp.exp(z)))
        if masked:
            sp = jnp.where(causal, sp, 0.0)
        ts = [None] * nsub
        for u in reversed(range(nsub)):
            lo_row, hi_row = u * SB_SUB, (u + 1) * SB_SUB
            tail = sp[hi_row - SUBLANES:hi_row] + jnp.where(last_sublane, carry, 0.0)
            spu = jnp.concatenate([sp[lo_row:hi_row - SUBLANES], tail], axis=0)
            hi = lax.bitcast_convert_type(lax.bitcast_convert_type(spu, jnp.uint32) & jnp.uint32(HI_MASK), F32)
            operand = jnp.concatenate([hi.astype(BF16), (spu - hi).astype(BF16)], axis=0)
            inc = jnp.dot(incl2, operand, preferred_element_type=F32)
            ts[u] = z[lo_row:hi_row] - inc
            if u > 0:
                carry = carry + jnp.sum(sp[lo_row:hi_row], axis=0, keepdims=True)
            else:
                carry = inc[0:1, :]
        a = jnp.exp(jnp.concatenate(ts, axis=0).astype(BF16))
        if masked:
            a = jnp.where(causal, a, jnp.zeros_like(a))
        a_ref[...] = a
        carry_ref[...] = carry

    def values(j, a_ref):
        for h in range(SB_GROUP):
            vt = vt_ref[0, j, h * HEAD_DIM:(h + 1) * HEAD_DIM, :]
            acc_ref[h * HEAD_DIM:(h + 1) * HEAD_DIM, :] += jnp.dot(
                vt, a_ref[:, h * blk:(h + 1) * blk], preferred_element_type=F32)

    carry_ref[...] = jnp.zeros_like(carry_ref)
    acc_ref[...] = jnp.zeros_like(acc_ref)
    scores(i, za_ref)
    weights(za_ref, wb_ref, True)

    @pl.when(i % 2 == 1)
    def _():
        scores(i - 1, za_ref)
        values(i, wb_ref)
        weights(za_ref, wb_ref, False)

    first = i - 1 - i % 2
    scores(jnp.maximum(first, 0), za_ref)

    def pair(p, _):
        j = first - 2 * p
        scores(j - 1, zb_ref)
        values(j + 1, wb_ref)
        weights(za_ref, wa_ref, False)
        scores(jnp.maximum(j - 2, 0), za_ref)
        values(j, wa_ref)
        weights(zb_ref, wb_ref, False)
        return 0

    lax.fori_loop(0, i // 2, pair, 0)
    values(0, wb_ref)
    o_ref[...] = acc_ref[...].T.astype(BF16)


def _sb_attention(qkv, vt, batch, seq, blk=256):
    nq = seq // blk
    width = SB_GROUP * HEAD_DIM
    groups = SB_W // width
    return pl.pallas_call(
        functools.partial(_sb_kernel, blk=blk),
        grid=(batch, groups, nq),
        in_specs=[
            pl.BlockSpec((blk, width), lambda b, g, i: (b * nq + i, g)),
            pl.BlockSpec((seq, width), lambda b, g, i: (b, groups + g)),
            pl.BlockSpec((1, nq, width, blk), lambda b, g, i: (b, 0, g, 0)),
        ],
        out_specs=pl.BlockSpec((blk, width), lambda b, g, i: (b * nq + i, g)),
        out_shape=jax.ShapeDtypeStruct((batch * seq, SB_W), BF16),
        scratch_shapes=[
            pltpu.VMEM((blk, SB_GROUP * blk), F32),
            pltpu.VMEM((blk, SB_GROUP * blk), F32),
            pltpu.VMEM((blk, SB_GROUP * blk), BF16),
            pltpu.VMEM((blk, SB_GROUP * blk), BF16),
            pltpu.VMEM((1, SB_GROUP * blk), F32),
            pltpu.VMEM((width, blk), F32),
        ],
        compiler_params=_params("arbitrary", "arbitrary", "arbitrary"),
        name="sb_attn",
    )(qkv, qkv, vt)


def _swa_kernel(sink_ref, q_ref, kp_ref, kc_ref, vp_ref, vc_ref, o_ref, *, slopes):
    n = pl.program_id(1)
    blk = q_ref.shape[0]
    q = q_ref[...] * BF16(1.0 / math.sqrt(HEAD_DIM))
    r = lax.broadcasted_iota(jnp.int32, (blk, 2 * blk), 0)
    c = lax.broadcasted_iota(jnp.int32, (blk, 2 * blk), 1)
    dist = r - c + blk
    mask = (dist >= 0) & (dist < WINDOW) & ((c >= blk) | (n > 0))
    distf = dist.astype(F32)
    kb = jnp.concatenate([kp_ref[...], kc_ref[...]], axis=0)
    vb = jnp.concatenate([vp_ref[...], vc_ref[...]], axis=0)
    outs = []
    for h in range(SWA_Q_HEADS):
        g = h // SWA_GROUP
        qh = q[:, h * HEAD_DIM:(h + 1) * HEAD_DIM]
        kg = kb[:, g * HEAD_DIM:(g + 1) * HEAD_DIM]
        vg = vb[:, g * HEAD_DIM:(g + 1) * HEAD_DIM]
        z = lax.dot_general(qh, kg, (((1,), (1,)), ((), ())), preferred_element_type=F32)
        z = jnp.where(mask, z - slopes[h] * distf, NEG_INF)
        sink = sink_ref[h]
        zmax = jnp.maximum(jnp.max(z, axis=-1, keepdims=True), sink)
        p = jnp.exp(z - zmax)
        denom = jnp.sum(p, axis=-1, keepdims=True) + jnp.exp(sink - zmax)
        o = jnp.dot(p.astype(BF16), vg, preferred_element_type=F32)
        outs.append(o / denom)
    o_ref[...] = jnp.concatenate(outs, axis=1).astype(BF16)


def _swa_attention(qkv, sinks, batch, seq, blk=128):
    nb = seq // blk
    qcol = 3 * SB_W // SWA_Q_W
    kcol = (3 * SB_W + SWA_Q_W) // SWA_KV_W
    vcol = kcol + 1
    slopes = [float(v) for v in np.power(2.0, -8.0 * np.arange(1, SWA_Q_HEADS + 1) / SWA_Q_HEADS).astype(np.float32)]
    cur = lambda col: (lambda b, n: (b * nb + n, col))
    prev = lambda col: (lambda b, n: (b * nb + jnp.maximum(n - 1, 0), col))
    return pl.pallas_call(
        functools.partial(_swa_kernel, slopes=slopes),
        grid=(batch, nb),
        in_specs=[
            pl.BlockSpec(memory_space=pltpu.SMEM),
            pl.BlockSpec((blk, SWA_Q_W), cur(qcol)),
            pl.BlockSpec((blk, SWA_KV_W), prev(kcol)),
            pl.BlockSpec((blk, SWA_KV_W), cur(kcol)),
            pl.BlockSpec((blk, SWA_KV_W), prev(vcol)),
            pl.BlockSpec((blk, SWA_KV_W), cur(vcol)),
        ],
        out_specs=pl.BlockSpec((blk, SWA_Q_W), lambda b, n: (b * nb + n, 0)),
        out_shape=jax.ShapeDtypeStruct((batch * seq, SWA_Q_W), BF16),
        compiler_params=_params("arbitrary", "arbitrary"),
        name="swa_attn",
    )(sinks, qkv, qkv, qkv, qkv, qkv)


def _mix_kernel(xn_ref, wga_ref, wgb_ref, a_ref, wa_ref, b_ref, wb_ref, o_ref):
    xn = xn_ref[...]
    ga = jnp.dot(xn, wga_ref[...], preferred_element_type=F32)
    gb = jnp.dot(xn, wgb_ref[...], preferred_element_type=F32)
    ya = jnp.dot(a_ref[...], wa_ref[...], preferred_element_type=F32)
    yb = jnp.dot(b_ref[...], wb_ref[...], preferred_element_type=F32)
    o_ref[...] = (jax.nn.sigmoid(ga) * ya + jax.nn.sigmoid(gb) * yb).astype(BF16)


def _mix(xn, w_ga, w_gb, att_a, w_a, att_b, w_b, tm=1024, tn=512):
    t = xn.shape[0]
    return pl.pallas_call(
        _mix_kernel,
        grid=(t // tm, D_MODEL // tn),
        in_specs=[
            pl.BlockSpec((tm, D_MODEL), lambda m, n: (m, 0)),
            pl.BlockSpec((D_MODEL, tn), lambda m, n: (0, n)),
            pl.BlockSpec((D_MODEL, tn), lambda m, n: (0, n)),
            pl.BlockSpec((tm, SB_W), lambda m, n: (m, 0)),
            pl.BlockSpec((SB_W, tn), lambda m, n: (0, n)),
            pl.BlockSpec((tm, SWA_Q_W), lambda m, n: (m, 0)),
            pl.BlockSpec((SWA_Q_W, tn), lambda m, n: (0, n)),
        ],
        out_specs=pl.BlockSpec((tm, tn), lambda m, n: (m, n)),
        out_shape=jax.ShapeDtypeStruct((t, D_MODEL), BF16),
        compiler_params=_params("arbitrary", "arbitrary"),
        name="mix",
    )(xn, w_ga, w_gb, att_a, w_a, att_b, w_b)


def _oproj_kernel(x_ref, mixed_ref, w_ref, g_ref, h_ref, xn_ref):
    h = x_ref[...] + jnp.dot(mixed_ref[...], w_ref[...], preferred_element_type=F32)
    h_ref[...] = h
    xn_ref[...] = _rms(h, g_ref[...]).astype(BF16)


def _oproj(x2, mixed, w_o, g, tm=512):
    t = x2.shape[0]
    return pl.pallas_call(
        _oproj_kernel,
        grid=(t // tm,),
        in_specs=[
            pl.BlockSpec((tm, D_MODEL), lambda m: (m, 0)),
            pl.BlockSpec((tm, D_MODEL), lambda m: (m, 0)),
            _resident((D_MODEL, D_MODEL), lambda m: (0, 0)),
            _resident((1, D_MODEL), lambda m: (0, 0)),
        ],
        out_specs=[
            pl.BlockSpec((tm, D_MODEL), lambda m: (m, 0)),
            pl.BlockSpec((tm, D_MODEL), lambda m: (m, 0)),
        ],
        out_shape=[
            jax.ShapeDtypeStruct((t, D_MODEL), F32),
            jax.ShapeDtypeStruct((t, D_MODEL), BF16),
        ],
        compiler_params=_params("arbitrary"),
        name="oproj",
    )(x2, mixed, w_o, g)


CONV_HALO = 8
FFN_ROWS = 256


def _ffn_up_kernel(xn_ref, wg_ref, wu_ref, cg_ref, cu_ref, bg_ref, bu_ref, o_ref, hg_ref, hu_ref, wgb_ref, wub_ref, *,
                   tiles_per_seq):
    m = pl.program_id(1)
    tm = xn_ref.shape[0]
    for h_ref in (hg_ref, hu_ref):
        @pl.when(m % tiles_per_seq == 0)
        def _():
            h_ref[0:CONV_HALO, :] = jnp.zeros((CONV_HALO, h_ref.shape[1]), F32)

        @pl.when(m % tiles_per_seq != 0)
        def _():
            h_ref[0:CONV_HALO, :] = h_ref[tm:tm + CONV_HALO, :]

    def conv(h_ref, c_ref, b_ref, r0):
        h = h_ref[r0:r0 + CONV_HALO + FFN_ROWS, :]
        out = b_ref[...] + c_ref[CONV_WIDTH - 1:CONV_WIDTH, :] * h[CONV_HALO:]
        for back in range(1, CONV_WIDTH):
            shifted = pltpu.roll(h, back, axis=0)[CONV_HALO:]
            out = out + c_ref[CONV_WIDTH - 1 - back:CONV_WIDTH - back, :] * shifted
        return out

    @pl.when(m == 0)
    def _():
        wgb_ref[...] = wg_ref[...].astype(BF16)
        wub_ref[...] = wu_ref[...].astype(BF16)

    for r0 in range(0, tm, FFN_ROWS):
        xn = xn_ref[r0:r0 + FFN_ROWS, :]
        for h_ref, w_ref in ((hg_ref, wgb_ref), (hu_ref, wub_ref)):
            h_ref[CONV_HALO + r0:CONV_HALO + r0 + FFN_ROWS, :] = jnp.dot(
                xn, w_ref[...], preferred_element_type=F32)
        gate = conv(hg_ref, cg_ref, bg_ref, r0)
        up = conv(hu_ref, cu_ref, bu_ref, r0)
        o_ref[r0:r0 + FFN_ROWS, :] = (gate * jax.nn.sigmoid(gate) * up).astype(BF16)


def _ffn_up(xn2, w_up, conv_w, conv_b, seq, tm=1024, tn=512):
    t = xn2.shape[0]
    nn = D_FF // tn
    kern = functools.partial(_ffn_up_kernel, tiles_per_seq=seq // tm)
    return pl.pallas_call(
        kern,
        grid=(nn, t // tm),
        in_specs=[
            pl.BlockSpec((tm, D_MODEL), lambda n, m: (m, 0)),
            pl.BlockSpec((D_MODEL, tn), lambda n, m: (0, n)),
            pl.BlockSpec((D_MODEL, tn), lambda n, m: (0, nn + n)),
            pl.BlockSpec((CONV_WIDTH, tn), lambda n, m: (0, n)),
            pl.BlockSpec((CONV_WIDTH, tn), lambda n, m: (0, nn + n)),
            pl.BlockSpec((1, tn), lambda n, m: (0, n)),
            pl.BlockSpec((1, tn), lambda n, m: (0, nn + n)),
        ],
        out_specs=pl.BlockSpec((tm, tn), lambda n, m: (m, n)),
        out_shape=jax.ShapeDtypeStruct((t, D_FF), BF16),
        scratch_shapes=[
            pltpu.VMEM((CONV_HALO + tm, tn), F32),
            pltpu.VMEM((CONV_HALO + tm, tn), F32),
            pltpu.VMEM((D_MODEL, tn), BF16),
            pltpu.VMEM((D_MODEL, tn), BF16),
        ],
        compiler_params=_params("arbitrary", "arbitrary"),
        name="ffn_up",
    )(xn2, w_up, w_up, conv_w, conv_w, conv_b, conv_b)


def _ffn_down_kernel(h_ref, act_ref, w_ref, g_ref, o_ref):
    h = h_ref[...] + jnp.dot(act_ref[...], w_ref[...], preferred_element_type=F32)
    o_ref[...] = _rms(h, g_ref[...])


def _ffn_down(h1, act, w_down, g, tm=256):
    t = h1.shape[0]
    return pl.pallas_call(
        _ffn_down_kernel,
        grid=(t // tm,),
        in_specs=[
            pl.BlockSpec((tm, D_MODEL), lambda m: (m, 0)),
            pl.BlockSpec((tm, D_FF), lambda m: (m, 0)),
            _resident((D_FF, D_MODEL), lambda m: (0, 0)),
            _resident((1, D_MODEL), lambda m: (0, 0)),
        ],
        out_specs=pl.BlockSpec((tm, D_MODEL), lambda m: (m, 0)),
        out_shape=jax.ShapeDtypeStruct((t, D_MODEL), F32),
        compiler_params=_params("arbitrary"),
        name="ffn_down",
    )(h1, act, w_down, g)


def kernel(x, norm_mix_g, w_in, w_sb_out, w_swa_out, w_o, sinks, norm_ffn_g, w_up, conv_w, conv_b, w_down, norm_final_g):
    batch, seq, _ = x.shape
    assert w_in.shape[0] == 1, "single-layer block: the final norm is fused into the FFN down projection"
    sb_blk = 256
    row = lambda v: v.reshape(1, -1)
    x2 = x.reshape(batch * seq, D_MODEL)
    w_qkv = w_in[0, :, :QKV_W].astype(BF16)
    w_ga = w_in[0, :, QKV_W:QKV_W + D_MODEL].astype(BF16)
    w_gb = w_in[0, :, QKV_W + D_MODEL:].astype(BF16)
    xn, qkv = _inproj(x2, row(norm_mix_g[0]), w_qkv)
    v_sb = qkv[:, 2 * SB_W:3 * SB_W].reshape(batch, seq // sb_blk, sb_blk, SB_W)
    vt = jnp.transpose(v_sb, (0, 1, 3, 2))
    att_a = _sb_attention(qkv, vt, batch, seq, blk=sb_blk)
    att_b = _swa_attention(qkv, sinks[0], batch, seq)
    mixed = _mix(xn, w_ga, w_gb, att_a, w_sb_out[0].astype(BF16), att_b, w_swa_out[0].astype(BF16))
    h1, xn2 = _oproj(x2, mixed, w_o[0].astype(BF16), row(norm_ffn_g[0]))
    act = _ffn_up(xn2, w_up[0], conv_w[0], row(conv_b[0]), seq)
    out = _ffn_down(h1, act, w_down[0].astype(BF16), row(norm_final_g))
    return out.reshape(batch, seq, D_MODEL)
```

```python
import functools
import math
from typing import Callable, NamedTuple

import jax
import jax.numpy as jnp
import numpy as np
from jax import lax
from jax.experimental import pallas as pl
from jax.experimental.pallas import tpu as pltpu

D_MODEL = 2048
HEAD_DIM = 64
SB_HEADS = 16
SWA_Q_HEADS = 16
SWA_KV_HEADS = 2
SWA_GROUP = SWA_Q_HEADS // SWA_KV_HEADS
WINDOW = 128
D_FF = 5632
CONV_WIDTH = 3
RMS_EPS = 1e-5
NEG_INF = -1e30

SB_W = SB_HEADS * HEAD_DIM
SWA_Q_W = SWA_Q_HEADS * HEAD_DIM
SWA_KV_W = SWA_KV_HEADS * HEAD_DIM
QKV_W = 3 * SB_W + SWA_Q_W + 2 * SWA_KV_W
LANES = 128
SUBLANES = 8

BF16 = jnp.bfloat16
F32 = jnp.float32

VMEM_LIMIT = 56 * 1024 * 1024


def _params(*sem):
    return pltpu.CompilerParams(dimension_semantics=sem, vmem_limit_bytes=VMEM_LIMIT)


def _resident(shape, index_map):
    return pl.BlockSpec(shape, index_map, pipeline_mode=pl.Buffered(1))


def _rms(xf, g):
    return xf * lax.rsqrt(jnp.mean(xf * xf, axis=-1, keepdims=True) + RMS_EPS) * g


SB_BLK = 256


def _inproj_kernel(x_ref, g_ref, w_ref, xn_ref, qkv_ref, vt_ref):
    xn = _rms(x_ref[...], g_ref[...]).astype(BF16)
    xn_ref[...] = xn
    qkv = jnp.dot(xn, w_ref[...], preferred_element_type=F32)
    qkv_ref[...] = qkv.astype(BF16)
    vt = qkv[:, 2 * SB_W:3 * SB_W].T.astype(BF16)
    for kb in range(vt_ref.shape[1]):
        vt_ref[0, kb] = vt[:, kb * SB_BLK:(kb + 1) * SB_BLK]


def _inproj(x2, g, w_qkv, batch, seq, tm=512):
    t = x2.shape[0]
    tiles_per_seq = seq // tm
    return pl.pallas_call(
        _inproj_kernel,
        grid=(t // tm,),
        in_specs=[
            pl.BlockSpec((tm, D_MODEL), lambda m: (m, 0)),
            _resident((1, D_MODEL), lambda m: (0, 0)),
            _resident((D_MODEL, QKV_W), lambda m: (0, 0)),
        ],
        out_specs=[
            pl.BlockSpec((tm, D_MODEL), lambda m: (m, 0)),
            pl.BlockSpec((tm, QKV_W), lambda m: (m, 0)),
            pl.BlockSpec((1, tm // SB_BLK, SB_W, SB_BLK), lambda m: (m // tiles_per_seq, m % tiles_per_seq, 0, 0)),
        ],
        out_shape=[
            jax.ShapeDtypeStruct((t, D_MODEL), BF16),
            jax.ShapeDtypeStruct((t, QKV_W), BF16),
            jax.ShapeDtypeStruct((batch, seq // SB_BLK, SB_W, SB_BLK), BF16),
        ],
        compiler_params=_params("arbitrary"),
        name="inproj",
    )(x2, g, w_qkv)


SB_GROUP = 8
SB_CHUNK = 256
SB_SUB = 128
HI_MASK = 0xFFFF0000


def _sb_kernel(q_ref, k_ref, vt_ref, *rest, blk, n_casts):
    cast_srcs, o_ref, cast_dsts = rest[:n_casts], rest[n_casts], rest[n_casts + 1:2 * n_casts + 1]
    za_ref, zb_ref, wa_ref, wb_ref, carry_ref, acc_ref = rest[2 * n_casts + 1:]
    for src, dst in zip(cast_srcs, cast_dsts):
        dst[...] = src[...].astype(BF16)
    i = pl.program_id(2)
    width = SB_GROUP * HEAD_DIM
    n = SB_GROUP * blk
    nsub = blk // SB_SUB
    q = q_ref[...] * BF16(1.0 / math.sqrt(HEAD_DIM))
    qt = q.astype(F32).T.astype(BF16)
    chunk_heads = SB_CHUNK // HEAD_DIM
    head_of_row = lax.broadcasted_iota(jnp.int32, (SB_CHUNK, blk), 0) // HEAD_DIM
    qstacks = []
    for cc in range(width // SB_CHUNK):
        qt_c = qt[cc * SB_CHUNK:(cc + 1) * SB_CHUNK]
        qstacks.append(jnp.concatenate(
            [jnp.where(head_of_row == c, qt_c, jnp.zeros_like(qt_c)) for c in range(chunk_heads)], axis=1))
    r = lax.broadcasted_iota(jnp.int32, (SB_SUB, SB_SUB), 0)
    c = lax.broadcasted_iota(jnp.int32, (SB_SUB, SB_SUB), 1)
    incl = (c >= r).astype(BF16)
    incl2 = jnp.concatenate([incl, incl], axis=1)
    key_idx = lax.broadcasted_iota(jnp.int32, (blk, n), 0)
    query_idx = lax.broadcasted_iota(jnp.int32, (blk, n), 1) & (blk - 1)
    causal = key_idx < query_idx
    last_sublane = lax.broadcasted_iota(jnp.int32, (SUBLANES, n), 0) == SUBLANES - 1

    def scores(j, z_ref):
        kb = k_ref[pl.ds(j * blk, blk), :]
        for cc, qs in enumerate(qstacks):
            z_ref[:, cc * chunk_heads * blk:(cc + 1) * chunk_heads * blk] = jnp.dot(
                kb[:, cc * SB_CHUNK:(cc + 1) * SB_CHUNK], qs, preferred_element_type=F32)

    def weights(z_ref, a_ref, masked):
        z = z_ref[...]
        carry = carry_ref[...]
        sp = jnp.where(z > 17.0, z, jnp.log(1.0 + jnp.exp(z)))
        if masked:
            sp = jnp.where(causal, sp, 0.0)
        ts = [None] * nsub
        for u in reversed(range(nsub)):
            lo_row, hi_row = u * SB_SUB, (u + 1) * SB_SUB
            tail = sp[hi_row - SUBLANES:hi_row] + jnp.where(last_sublane, carry, 0.0)
            spu = jnp.concatenate([sp[lo_row:hi_row - SUBLANES], tail], axis=0)
            hi = lax.bitcast_convert_type(lax.bitcast_convert_type(spu, jnp.uint32) & jnp.uint32(HI_MASK), F32)
            operand = jnp.concatenate([hi.astype(BF16), (spu - hi).astype(BF16)], axis=0)
            inc = jnp.dot(incl2, operand, preferred_element_type=F32)
            ts[u] = z[lo_row:hi_row] - inc
            if u > 0:
                carry = carry + jnp.sum(sp[lo_row:hi_row], axis=0, keepdims=True)
            else:
                carry = inc[0:1, :]
        a = jnp.exp(jnp.concatenate(ts, axis=0).astype(BF16))
        if masked:
            a = jnp.where(causal, a, jnp.zeros_like(a))
        a_ref[...] = a
        carry_ref[...] = carry

    def values(j, a_ref):
        for h in range(SB_GROUP):
            vt = vt_ref[0, j, h * HEAD_DIM:(h + 1) * HEAD_DIM, :]
            acc_ref[h * HEAD_DIM:(h + 1) * HEAD_DIM, :] += jnp.dot(
                vt, a_ref[:, h * blk:(h + 1) * blk], preferred_element_type=F32)

    carry_ref[...] = jnp.zeros_like(carry_ref)
    acc_ref[...] = jnp.zeros_like(acc_ref)
    scores(i, za_ref)
    weights(za_ref, wb_ref, True)

    @pl.when(i % 2 == 1)
    def _():
        scores(i - 1, za_ref)
        values(i, wb_ref)
        weights(za_ref, wb_ref, False)

    first = i - 1 - i % 2
    scores(jnp.maximum(first, 0), za_ref)

    def pair(p, _):
        j = first - 2 * p
        scores(j - 1, zb_ref)
        values(j + 1, wb_ref)
        weights(za_ref, wa_ref, False)
        scores(jnp.maximum(j - 2, 0), za_ref)
        values(j, wa_ref)
        weights(zb_ref, wb_ref, False)
        return 0

    lax.fori_loop(0, i // 2, pair, 0)
    values(0, wb_ref)
    o_ref[...] = acc_ref[...].T.astype(BF16)


class _CastJob(NamedTuple):
    src: jax.Array
    block: tuple
    src_index: Callable
    out_shape: tuple
    out_index: Callable


def _row_cast(src, steps):
    rows, cols = src.shape
    assert rows % (steps * 16) == 0, (rows, steps)
    return _CastJob(src, (rows // steps, cols), lambda s: (s, 0), (rows, cols), lambda s: (s, 0))


def _sb_attention(qkv, vt, batch, seq, cast_jobs, blk=SB_BLK):
    nq = seq // blk
    width = SB_GROUP * HEAD_DIM
    groups = SB_W // width
    step = lambda b, g, i: (b * groups + g) * nq + i
    at_step = lambda index: (lambda b, g, i: index(step(b, g, i)))
    outs = pl.pallas_call(
        functools.partial(_sb_kernel, blk=blk, n_casts=len(cast_jobs)),
        grid=(batch, groups, nq),
        in_specs=[
            pl.BlockSpec((blk, width), lambda b, g, i: (b * nq + i, g)),
            pl.BlockSpec((seq, width), lambda b, g, i: (b, groups + g)),
            pl.BlockSpec((1, nq, width, blk), lambda b, g, i: (b, 0, g, 0)),
        ] + [pl.BlockSpec(job.block, at_step(job.src_index)) for job in cast_jobs],
        out_specs=[pl.BlockSpec((blk, width), lambda b, g, i: (b * nq + i, g))]
        + [pl.BlockSpec(job.block, at_step(job.out_index)) for job in cast_jobs],
        out_shape=[jax.ShapeDtypeStruct((batch * seq, SB_W), BF16)]
        + [jax.ShapeDtypeStruct(job.out_shape, BF16) for job in cast_jobs],
        scratch_shapes=[
            pltpu.VMEM((blk, SB_GROUP * blk), F32),
            pltpu.VMEM((blk, SB_GROUP * blk), F32),
            pltpu.VMEM((blk, SB_GROUP * blk), BF16),
            pltpu.VMEM((blk, SB_GROUP * blk), BF16),
            pltpu.VMEM((1, SB_GROUP * blk), F32),
            pltpu.VMEM((width, blk), F32),
        ],
        compiler_params=_params("arbitrary", "arbitrary", "arbitrary"),
        name="sb_attn",
    )(qkv, qkv, vt, *[job.src for job in cast_jobs])
    return outs[0], outs[1:]


def _swa_kernel(sink_ref, q_ref, kp_ref, kc_ref, vp_ref, vc_ref, o_ref, *, slopes):
    n = pl.program_id(1)
    blk = q_ref.shape[0]
    q = q_ref[...] * BF16(1.0 / math.sqrt(HEAD_DIM))
    r = lax.broadcasted_iota(jnp.int32, (blk, 2 * blk), 0)
    c = lax.broadcasted_iota(jnp.int32, (blk, 2 * blk), 1)
    dist = r - c + blk
    mask = (dist >= 0) & (dist < WINDOW) & ((c >= blk) | (n > 0))
    distf = dist.astype(F32)
    kb = jnp.concatenate([kp_ref[...], kc_ref[...]], axis=0)
    vb = jnp.concatenate([vp_ref[...], vc_ref[...]], axis=0)
    outs = []
    for h in range(SWA_Q_HEADS):
        g = h // SWA_GROUP
        qh = q[:, h * HEAD_DIM:(h + 1) * HEAD_DIM]
        kg = kb[:, g * HEAD_DIM:(g + 1) * HEAD_DIM]
        vg = vb[:, g * HEAD_DIM:(g + 1) * HEAD_DIM]
        z = lax.dot_general(qh, kg, (((1,), (1,)), ((), ())), preferred_element_type=F32)
        z = jnp.where(mask, z - slopes[h] * distf, NEG_INF)
        sink = sink_ref[h]
        zmax = jnp.maximum(jnp.max(z, axis=-1, keepdims=True), sink)
        p = jnp.exp(z - zmax)
        denom = jnp.sum(p, axis=-1, keepdims=True) + jnp.exp(sink - zmax)
        o = jnp.dot(p.astype(BF16), vg, preferred_element_type=F32)
        outs.append(o / denom)
    o_ref[...] = jnp.concatenate(outs, axis=1).astype(BF16)


def _swa_attention(qkv, sinks, batch, seq, blk=128):
    nb = seq // blk
    qcol = 3 * SB_W // SWA_Q_W
    kcol = (3 * SB_W + SWA_Q_W) // SWA_KV_W
    vcol = kcol + 1
    slopes = [float(v) for v in np.power(2.0, -8.0 * np.arange(1, SWA_Q_HEADS + 1) / SWA_Q_HEADS).astype(np.float32)]
    cur = lambda col: (lambda b, n: (b * nb + n, col))
    prev = lambda col: (lambda b, n: (b * nb + jnp.maximum(n - 1, 0), col))
    return pl.pallas_call(
        functools.partial(_swa_kernel, slopes=slopes),
        grid=(batch, nb),
        in_specs=[
            pl.BlockSpec(memory_space=pltpu.SMEM),
            pl.BlockSpec((blk, SWA_Q_W), cur(qcol)),
            pl.BlockSpec((blk, SWA_KV_W), prev(kcol)),
            pl.BlockSpec((blk, SWA_KV_W), cur(kcol)),
            pl.BlockSpec((blk, SWA_KV_W), prev(vcol)),
            pl.BlockSpec((blk, SWA_KV_W), cur(vcol)),
        ],
        out_specs=pl.BlockSpec((blk, SWA_Q_W), lambda b, n: (b * nb + n, 0)),
        out_shape=jax.ShapeDtypeStruct((batch * seq, SWA_Q_W), BF16),
        compiler_params=_params("arbitrary", "arbitrary"),
        name="swa_attn",
    )(sinks, qkv, qkv, qkv, qkv, qkv)


def _mix_kernel(xn_ref, wga_ref, wgb_ref, a_ref, wa_ref, b_ref, wb_ref, o_ref):
    xn = xn_ref[...]
    ga = jnp.dot(xn, wga_ref[...], preferred_element_type=F32)
    gb = jnp.dot(xn, wgb_ref[...], preferred_element_type=F32)
    ya = jnp.dot(a_ref[...], wa_ref[...], preferred_element_type=F32)
    yb = jnp.dot(b_ref[...], wb_ref[...], preferred_element_type=F32)
    o_ref[...] = (jax.nn.sigmoid(ga) * ya + jax.nn.sigmoid(gb) * yb).astype(BF16)


def _mix(xn, w_gates, att_a, w_a, att_b, w_b, tm=1024, tn=512):
    t = xn.shape[0]
    nn = D_MODEL // tn
    return pl.pallas_call(
        _mix_kernel,
        grid=(t // tm, nn),
        in_specs=[
            pl.BlockSpec((tm, D_MODEL), lambda m, n: (m, 0)),
            pl.BlockSpec((D_MODEL, tn), lambda m, n: (0, n)),
            pl.BlockSpec((D_MODEL, tn), lambda m, n: (0, nn + n)),
            pl.BlockSpec((tm, SB_W), lambda m, n: (m, 0)),
            pl.BlockSpec((SB_W, tn), lambda m, n: (0, n)),
            pl.BlockSpec((tm, SWA_Q_W), lambda m, n: (m, 0)),
            pl.BlockSpec((SWA_Q_W, tn), lambda m, n: (0, n)),
        ],
        out_specs=pl.BlockSpec((tm, tn), lambda m, n: (m, n)),
        out_shape=jax.ShapeDtypeStruct((t, D_MODEL), BF16),
        compiler_params=_params("arbitrary", "arbitrary"),
        name="mix",
    )(xn, w_gates, w_gates, att_a, w_a, att_b, w_b)


def _oproj_kernel(x_ref, mixed_ref, w_ref, g_ref, h_ref, xn_ref):
    h = x_ref[...] + jnp.dot(mixed_ref[...], w_ref[...], preferred_element_type=F32)
    h_ref[...] = h
    xn_ref[...] = _rms(h, g_ref[...]).astype(BF16)


def _oproj(x2, mixed, w_o, g, tm=512):
    t = x2.shape[0]
    return pl.pallas_call(
        _oproj_kernel,
        grid=(t // tm,),
        in_specs=[
            pl.BlockSpec((tm, D_MODEL), lambda m: (m, 0)),
            pl.BlockSpec((tm, D_MODEL), lambda m: (m, 0)),
            _resident((D_MODEL, D_MODEL), lambda m: (0, 0)),
            _resident((1, D_MODEL), lambda m: (0, 0)),
        ],
        out_specs=[
            pl.BlockSpec((tm, D_MODEL), lambda m: (m, 0)),
            pl.BlockSpec((tm, D_MODEL), lambda m: (m, 0)),
        ],
        out_shape=[
            jax.ShapeDtypeStruct((t, D_MODEL), F32),
            jax.ShapeDtypeStruct((t, D_MODEL), BF16),
        ],
        compiler_params=_params("arbitrary"),
        name="oproj",
    )(x2, mixed, w_o, g)


CONV_HALO = 8
FFN_ROWS = 256


def _ffn_up_kernel(xn_ref, wg_ref, wu_ref, cg_ref, cu_ref, bg_ref, bu_ref, wd_ref, o_ref, wdb_ref, hg_ref, hu_ref,
                   wgb_ref, wub_ref, *, tiles_per_seq):
    m = pl.program_id(1)
    tm = xn_ref.shape[0]
    wdb_ref[...] = wd_ref[...].astype(BF16)
    for h_ref in (hg_ref, hu_ref):
        @pl.when(m % tiles_per_seq == 0)
        def _():
            h_ref[0:CONV_HALO, :] = jnp.zeros((CONV_HALO, h_ref.shape[1]), F32)

        @pl.when(m % tiles_per_seq != 0)
        def _():
            h_ref[0:CONV_HALO, :] = h_ref[tm:tm + CONV_HALO, :]

    def conv(h_ref, c_ref, b_ref, r0):
        h = h_ref[r0:r0 + CONV_HALO + FFN_ROWS, :]
        out = b_ref[...] + c_ref[CONV_WIDTH - 1:CONV_WIDTH, :] * h[CONV_HALO:]
        for back in range(1, CONV_WIDTH):
            shifted = pltpu.roll(h, back, axis=0)[CONV_HALO:]
            out = out + c_ref[CONV_WIDTH - 1 - back:CONV_WIDTH - back, :] * shifted
        return out

    @pl.when(m == 0)
    def _():
        wgb_ref[...] = wg_ref[...].astype(BF16)
        wub_ref[...] = wu_ref[...].astype(BF16)

    for r0 in range(0, tm, FFN_ROWS):
        xn = xn_ref[r0:r0 + FFN_ROWS, :]
        for h_ref, w_ref in ((hg_ref, wgb_ref), (hu_ref, wub_ref)):
            h_ref[CONV_HALO + r0:CONV_HALO + r0 + FFN_ROWS, :] = jnp.dot(
                xn, w_ref[...], preferred_element_type=F32)
        gate = conv(hg_ref, cg_ref, bg_ref, r0)
        up = conv(hu_ref, cu_ref, bu_ref, r0)
        o_ref[r0:r0 + FFN_ROWS, :] = (gate * jax.nn.sigmoid(gate) * up).astype(BF16)


def _ffn_up(xn2, w_up, conv_w, conv_b, w_down, seq, tm=1024, tn=512):
    t = xn2.shape[0]
    nn = D_FF // tn
    nm = t // tm
    job = _row_cast(w_down, nn * nm)
    kern = functools.partial(_ffn_up_kernel, tiles_per_seq=seq // tm)
    return pl.pallas_call(
        kern,
        grid=(nn, nm),
        in_specs=[
            pl.BlockSpec((tm, D_MODEL), lambda n, m: (m, 0)),
            pl.BlockSpec((D_MODEL, tn), lambda n, m: (0, n)),
            pl.BlockSpec((D_MODEL, tn), lambda n, m: (0, nn + n)),
            pl.BlockSpec((CONV_WIDTH, tn), lambda n, m: (0, n)),
            pl.BlockSpec((CONV_WIDTH, tn), lambda n, m: (0, nn + n)),
            pl.BlockSpec((1, tn), lambda n, m: (0, n)),
            pl.BlockSpec((1, tn), lambda n, m: (0, nn + n)),
            pl.BlockSpec(job.block, lambda n, m: job.src_index(n * nm + m)),
        ],
        out_specs=[
            pl.BlockSpec((tm, tn), lambda n, m: (m, n)),
            pl.BlockSpec(job.block, lambda n, m: job.out_index(n * nm + m)),
        ],
        out_shape=[jax.ShapeDtypeStruct((t, D_FF), BF16), jax.ShapeDtypeStruct(job.out_shape, BF16)],
        scratch_shapes=[
            pltpu.VMEM((CONV_HALO + tm, tn), F32),
            pltpu.VMEM((CONV_HALO + tm, tn), F32),
            pltpu.VMEM((D_MODEL, tn), BF16),
            pltpu.VMEM((D_MODEL, tn), BF16),
        ],
        compiler_params=_params("arbitrary", "arbitrary"),
        name="ffn_up",
    )(xn2, w_up, w_up, conv_w, conv_w, conv_b, conv_b, w_down)


def _ffn_down_kernel(h_ref, act_ref, w_ref, g_ref, o_ref):
    h = h_ref[...] + jnp.dot(act_ref[...], w_ref[...], preferred_element_type=F32)
    o_ref[...] = _rms(h, g_ref[...])


def _ffn_down(h1, act, w_down, g, tm=256):
    t = h1.shape[0]
    return pl.pallas_call(
        _ffn_down_kernel,
        grid=(t // tm,),
        in_specs=[
            pl.BlockSpec((tm, D_MODEL), lambda m: (m, 0)),
            pl.BlockSpec((tm, D_FF), lambda m: (m, 0)),
            _resident((D_FF, D_MODEL), lambda m: (0, 0)),
            _resident((1, D_MODEL), lambda m: (0, 0)),
        ],
        out_specs=pl.BlockSpec((tm, D_MODEL), lambda m: (m, 0)),
        out_shape=jax.ShapeDtypeStruct((t, D_MODEL), F32),
        compiler_params=_params("arbitrary"),
        name="ffn_down",
    )(h1, act, w_down, g)


def kernel(x, norm_mix_g, w_in, w_sb_out, w_swa_out, w_o, sinks, norm_ffn_g, w_up, conv_w, conv_b, w_down, norm_final_g):
    batch, seq, _ = x.shape
    assert w_in.shape[0] == 1, "single-layer block: the final norm is fused into the FFN down projection"
    row = lambda v: v.reshape(1, -1)
    x2 = x.reshape(batch * seq, D_MODEL)
    w_in2 = w_in.reshape(D_MODEL, -1)
    xn, qkv, vt = _inproj(x2, row(norm_mix_g[0]), w_in2[:, :QKV_W].astype(BF16), batch, seq)

    sb_steps = batch * (SB_W // (SB_GROUP * HEAD_DIM)) * (seq // SB_BLK)
    gate_rows, gate_cols = 512, 256
    col_blocks = 2 * D_MODEL // gate_cols
    assert QKV_W % gate_cols == 0 and (D_MODEL // gate_rows) * col_blocks == sb_steps
    gates_job = _CastJob(
        w_in2, (gate_rows, gate_cols),
        lambda s: (s // col_blocks, QKV_W // gate_cols + s % col_blocks),
        (D_MODEL, 2 * D_MODEL), lambda s: (s // col_blocks, s % col_blocks))
    jobs = [gates_job, _row_cast(w_sb_out.reshape(SB_W, D_MODEL), sb_steps),
            _row_cast(w_swa_out.reshape(SWA_Q_W, D_MODEL), sb_steps), _row_cast(w_o.reshape(D_MODEL, D_MODEL), sb_steps)]
    att_a, (w_gates, w_a, w_b, w_ob) = _sb_attention(qkv, vt, batch, seq, jobs)
    att_b = _swa_attention(qkv, sinks[0], batch, seq)
    mixed = _mix(xn, w_gates, att_a, w_a, att_b, w_b)
    h1, xn2 = _oproj(x2, mixed, w_ob, row(norm_ffn_g[0]))
    act, w_db = _ffn_up(xn2, w_up.reshape(D_MODEL, 2 * D_FF), conv_w[0], row(conv_b[0]), w_down.reshape(D_FF, D_MODEL), seq)
    out = _ffn_down(h1, act, w_db, row(norm_final_g))
    return out.reshape(batch, seq, D_MODEL)
```

```python
import functools
import math
from typing import Callable, NamedTuple

import jax
import jax.numpy as jnp
import numpy as np
from jax import lax
from jax.experimental import pallas as pl
from jax.experimental.pallas import tpu as pltpu

D_MODEL = 2048
HEAD_DIM = 64
SB_HEADS = 16
SWA_Q_HEADS = 16
SWA_KV_HEADS = 2
SWA_GROUP = SWA_Q_HEADS // SWA_KV_HEADS
WINDOW = 128
D_FF = 5632
CONV_WIDTH = 3
RMS_EPS = 1e-5
NEG_INF = -1e30

SB_W = SB_HEADS * HEAD_DIM
SWA_Q_W = SWA_Q_HEADS * HEAD_DIM
SWA_KV_W = SWA_KV_HEADS * HEAD_DIM
QKV_W = 3 * SB_W + SWA_Q_W + 2 * SWA_KV_W
LANES = 128
SUBLANES = 8

BF16 = jnp.bfloat16
F32 = jnp.float32

VMEM_LIMIT = 56 * 1024 * 1024


def _params(*sem):
    return pltpu.CompilerParams(dimension_semantics=sem, vmem_limit_bytes=VMEM_LIMIT)


def _resident(shape, index_map):
    return pl.BlockSpec(shape, index_map, pipeline_mode=pl.Buffered(1))


def _rms(xf, g):
    return xf * lax.rsqrt(jnp.mean(xf * xf, axis=-1, keepdims=True) + RMS_EPS) * g


SB_BLK = 256


def _inproj_kernel(x_ref, g_ref, w_ref, xn_ref, qkv_ref, vt_ref):
    xn = _rms(x_ref[...], g_ref[...]).astype(BF16)
    xn_ref[...] = xn
    qkv = jnp.dot(xn, w_ref[...], preferred_element_type=F32)
    qkv_ref[...] = qkv.astype(BF16)
    vt = qkv[:, 2 * SB_W:3 * SB_W].T.astype(BF16)
    for kb in range(vt_ref.shape[1]):
        vt_ref[0, kb] = vt[:, kb * SB_BLK:(kb + 1) * SB_BLK]


def _inproj(x2, g, w_qkv, batch, seq, tm=512):
    t = x2.shape[0]
    tiles_per_seq = seq // tm
    return pl.pallas_call(
        _inproj_kernel,
        grid=(t // tm,),
        in_specs=[
            pl.BlockSpec((tm, D_MODEL), lambda m: (m, 0)),
            _resident((1, D_MODEL), lambda m: (0, 0)),
            _resident((D_MODEL, QKV_W), lambda m: (0, 0)),
        ],
        out_specs=[
            pl.BlockSpec((tm, D_MODEL), lambda m: (m, 0)),
            pl.BlockSpec((tm, QKV_W), lambda m: (m, 0)),
            pl.BlockSpec((1, tm // SB_BLK, SB_W, SB_BLK), lambda m: (m // tiles_per_seq, m % tiles_per_seq, 0, 0)),
        ],
        out_shape=[
            jax.ShapeDtypeStruct((t, D_MODEL), BF16),
            jax.ShapeDtypeStruct((t, QKV_W), BF16),
            jax.ShapeDtypeStruct((batch, seq // SB_BLK, SB_W, SB_BLK), BF16),
        ],
        compiler_params=_params("arbitrary"),
        name="inproj",
    )(x2, g, w_qkv)


SB_GROUP = 8
SB_CHUNK = 256
SB_SUB = 128
HI_MASK = 0xFFFF0000


def _sb_kernel(q_ref, k_ref, vt_ref, *rest, blk, n_casts):
    cast_srcs, o_ref, cast_dsts = rest[:n_casts], rest[n_casts], rest[n_casts + 1:2 * n_casts + 1]
    za_ref, zb_ref, wa_ref, wb_ref, carry_ref, acc_ref = rest[2 * n_casts + 1:]
    for src, dst in zip(cast_srcs, cast_dsts):
        dst[...] = src[...].astype(BF16)
    i = pl.program_id(2)
    width = SB_GROUP * HEAD_DIM
    n = SB_GROUP * blk
    nsub = blk // SB_SUB
    q = q_ref[...] * BF16(1.0 / math.sqrt(HEAD_DIM))
    qt = q.astype(F32).T.astype(BF16)
    chunk_heads = SB_CHUNK // HEAD_DIM
    head_of_row = lax.broadcasted_iota(jnp.int32, (SB_CHUNK, blk), 0) // HEAD_DIM
    qstacks = []
    for cc in range(width // SB_CHUNK):
        qt_c = qt[cc * SB_CHUNK:(cc + 1) * SB_CHUNK]
        qstacks.append(jnp.concatenate(
            [jnp.where(head_of_row == c, qt_c, jnp.zeros_like(qt_c)) for c in range(chunk_heads)], axis=1))
    r = lax.broadcasted_iota(jnp.int32, (SB_SUB, SB_SUB), 0)
    c = lax.broadcasted_iota(jnp.int32, (SB_SUB, SB_SUB), 1)
    incl = (c >= r).astype(BF16)
    incl2 = jnp.concatenate([incl, incl], axis=1)
    key_idx = lax.broadcasted_iota(jnp.int32, (blk, n), 0)
    query_idx = lax.broadcasted_iota(jnp.int32, (blk, n), 1) & (blk - 1)
    causal = key_idx < query_idx
    last_sublane = lax.broadcasted_iota(jnp.int32, (SUBLANES, n), 0) == SUBLANES - 1

    def scores(j, z_ref):
        kb = k_ref[pl.ds(j * blk, blk), :]
        for cc, qs in enumerate(qstacks):
            z_ref[:, cc * chunk_heads * blk:(cc + 1) * chunk_heads * blk] = jnp.dot(
                kb[:, cc * SB_CHUNK:(cc + 1) * SB_CHUNK], qs, preferred_element_type=F32)

    def weights(z_ref, a_ref, masked):
        z = z_ref[...]
        carry = carry_ref[...]
        sp = jnp.where(z > 17.0, z, jnp.log(1.0 + jnp.exp(z)))
        if masked:
            sp = jnp.where(causal, sp, 0.0)
        ts = [None] * nsub
        for u in reversed(range(nsub)):
            lo_row, hi_row = u * SB_SUB, (u + 1) * SB_SUB
            tail = sp[hi_row - SUBLANES:hi_row] + jnp.where(last_sublane, carry, 0.0)
            spu = jnp.concatenate([sp[lo_row:hi_row - SUBLANES], tail], axis=0)
            hi = lax.bitcast_convert_type(lax.bitcast_convert_type(spu, jnp.uint32) & jnp.uint32(HI_MASK), F32)
            operand = jnp.concatenate([hi.astype(BF16), (spu - hi).astype(BF16)], axis=0)
            inc = jnp.dot(incl2, operand, preferred_element_type=F32)
            ts[u] = z[lo_row:hi_row] - inc
            carry = inc[0:1, :]
        a = jnp.exp(jnp.concatenate(ts, axis=0).astype(BF16))
        if masked:
            a = jnp.where(causal, a, jnp.zeros_like(a))
        a_ref[...] = a
        carry_ref[...] = carry

    def values(j, a_ref):
        for h in range(SB_GROUP):
            vt = vt_ref[0, j, h * HEAD_DIM:(h + 1) * HEAD_DIM, :]
            acc_ref[h * HEAD_DIM:(h + 1) * HEAD_DIM, :] += jnp.dot(
                vt, a_ref[:, h * blk:(h + 1) * blk], preferred_element_type=F32)

    carry_ref[...] = jnp.zeros_like(carry_ref)
    acc_ref[...] = jnp.zeros_like(acc_ref)
    first = i - 1 - i % 2

    @pl.when(i % 2 == 0)
    def _():
        scores(i, zb_ref)
        scores(jnp.maximum(first, 0), za_ref)
        weights(zb_ref, wb_ref, True)

    @pl.when(i % 2 == 1)
    def _():
        scores(i, za_ref)
        scores(i - 1, zb_ref)
        weights(za_ref, wa_ref, True)
        scores(jnp.maximum(first, 0), za_ref)
        values(i, wa_ref)
        weights(zb_ref, wb_ref, False)

    def pair(p, _):
        j = first - 2 * p
        scores(j - 1, zb_ref)
        values(j + 1, wb_ref)
        weights(za_ref, wa_ref, False)
        scores(jnp.maximum(j - 2, 0), za_ref)
        values(j, wa_ref)
        weights(zb_ref, wb_ref, False)
        return 0

    lax.fori_loop(0, i // 2, pair, 0)
    values(0, wb_ref)
    o_ref[...] = acc_ref[...].T.astype(BF16)


class _CastJob(NamedTuple):
    src: jax.Array
    block: tuple
    src_index: Callable
    out_shape: tuple
    out_index: Callable


def _row_cast(src, steps):
    rows, cols = src.shape
    assert rows % (steps * 16) == 0, (rows, steps)
    return _CastJob(src, (rows // steps, cols), lambda s: (s, 0), (rows, cols), lambda s: (s, 0))


def _sb_attention(qkv, vt, batch, seq, cast_jobs, blk=SB_BLK):
    nq = seq // blk
    width = SB_GROUP * HEAD_DIM
    groups = SB_W // width
    step = lambda b, g, i: (b * groups + g) * nq + i
    at_step = lambda index: (lambda b, g, i: index(step(b, g, i)))
    outs = pl.pallas_call(
        functools.partial(_sb_kernel, blk=blk, n_casts=len(cast_jobs)),
        grid=(batch, groups, nq),
        in_specs=[
            pl.BlockSpec((blk, width), lambda b, g, i: (b * nq + i, g)),
            pl.BlockSpec((seq, width), lambda b, g, i: (b, groups + g)),
            pl.BlockSpec((1, nq, width, blk), lambda b, g, i: (b, 0, g, 0)),
        ] + [pl.BlockSpec(job.block, at_step(job.src_index)) for job in cast_jobs],
        out_specs=[pl.BlockSpec((blk, width), lambda b, g, i: (b * nq + i, g))]
        + [pl.BlockSpec(job.block, at_step(job.out_index)) for job in cast_jobs],
        out_shape=[jax.ShapeDtypeStruct((batch * seq, SB_W), BF16)]
        + [jax.ShapeDtypeStruct(job.out_shape, BF16) for job in cast_jobs],
        scratch_shapes=[
            pltpu.VMEM((blk, SB_GROUP * blk), F32),
            pltpu.VMEM((blk, SB_GROUP * blk), F32),
            pltpu.VMEM((blk, SB_GROUP * blk), BF16),
            pltpu.VMEM((blk, SB_GROUP * blk), BF16),
            pltpu.VMEM((1, SB_GROUP * blk), F32),
            pltpu.VMEM((width, blk), F32),
        ],
        compiler_params=_params("arbitrary", "arbitrary", "arbitrary"),
        name="sb_attn",
    )(qkv, qkv, vt, *[job.src for job in cast_jobs])
    return outs[0], outs[1:]


def _swa_kernel(sink_ref, q_ref, kp_ref, kc_ref, vp_ref, vc_ref, o_ref, *, slopes):
    n = pl.program_id(1)
    blk = q_ref.shape[0]
    q = q_ref[...] * BF16(1.0 / math.sqrt(HEAD_DIM))
    r = lax.broadcasted_iota(jnp.int32, (blk, 2 * blk), 0)
    c = lax.broadcasted_iota(jnp.int32, (blk, 2 * blk), 1)
    dist = r - c + blk
    mask = (dist >= 0) & (dist < WINDOW) & ((c >= blk) | (n > 0))
    distf = dist.astype(F32)
    kb = jnp.concatenate([kp_ref[...], kc_ref[...]], axis=0)
    vb = jnp.concatenate([vp_ref[...], vc_ref[...]], axis=0)
    outs = []
    for h in range(SWA_Q_HEADS):
        g = h // SWA_GROUP
        qh = q[:, h * HEAD_DIM:(h + 1) * HEAD_DIM]
        kg = kb[:, g * HEAD_DIM:(g + 1) * HEAD_DIM]
        vg = vb[:, g * HEAD_DIM:(g + 1) * HEAD_DIM]
        z = lax.dot_general(qh, kg, (((1,), (1,)), ((), ())), preferred_element_type=F32)
        z = jnp.where(mask, z - slopes[h] * distf, NEG_INF)
        sink = sink_ref[h]
        zmax = jnp.maximum(jnp.max(z, axis=-1, keepdims=True), sink)
        p = jnp.exp(z - zmax)
        denom = jnp.sum(p, axis=-1, keepdims=True) + jnp.exp(sink - zmax)
        o = jnp.dot(p.astype(BF16), vg, preferred_element_type=F32)
        outs.append(o / denom)
    o_ref[...] = jnp.concatenate(outs, axis=1).astype(BF16)


def _swa_attention(qkv, sinks, batch, seq, blk=128):
    nb = seq // blk
    qcol = 3 * SB_W // SWA_Q_W
    kcol = (3 * SB_W + SWA_Q_W) // SWA_KV_W
    vcol = kcol + 1
    slopes = [float(v) for v in np.power(2.0, -8.0 * np.arange(1, SWA_Q_HEADS + 1) / SWA_Q_HEADS).astype(np.float32)]
    cur = lambda col: (lambda b, n: (b * nb + n, col))
    prev = lambda col: (lambda b, n: (b * nb + jnp.maximum(n - 1, 0), col))
    return pl.pallas_call(
        functools.partial(_swa_kernel, slopes=slopes),
        grid=(batch, nb),
        in_specs=[
            pl.BlockSpec(memory_space=pltpu.SMEM),
            pl.BlockSpec((blk, SWA_Q_W), cur(qcol)),
            pl.BlockSpec((blk, SWA_KV_W), prev(kcol)),
            pl.BlockSpec((blk, SWA_KV_W), cur(kcol)),
            pl.BlockSpec((blk, SWA_KV_W), prev(vcol)),
            pl.BlockSpec((blk, SWA_KV_W), cur(vcol)),
        ],
        out_specs=pl.BlockSpec((blk, SWA_Q_W), lambda b, n: (b * nb + n, 0)),
        out_shape=jax.ShapeDtypeStruct((batch * seq, SWA_Q_W), BF16),
        compiler_params=_params("arbitrary", "arbitrary"),
        name="swa_attn",
    )(sinks, qkv, qkv, qkv, qkv, qkv)


def _mix_kernel(xn_ref, wga_ref, wgb_ref, a_ref, wa_ref, b_ref, wb_ref, o_ref):
    xn = xn_ref[...]
    ga = jnp.dot(xn, wga_ref[...], preferred_element_type=F32)
    gb = jnp.dot(xn, wgb_ref[...], preferred_element_type=F32)
    ya = jnp.dot(a_ref[...], wa_ref[...], preferred_element_type=F32)
    yb = jnp.dot(b_ref[...], wb_ref[...], preferred_element_type=F32)
    o_ref[...] = (jax.nn.sigmoid(ga) * ya + jax.nn.sigmoid(gb) * yb).astype(BF16)


def _mix(xn, w_gates, att_a, w_a, att_b, w_b, tm=1024, tn=512):
    t = xn.shape[0]
    nn = D_MODEL // tn
    return pl.pallas_call(
        _mix_kernel,
        grid=(t // tm, nn),
        in_specs=[
            pl.BlockSpec((tm, D_MODEL), lambda m, n: (m, 0)),
            pl.BlockSpec((D_MODEL, tn), lambda m, n: (0, n)),
            pl.BlockSpec((D_MODEL, tn), lambda m, n: (0, nn + n)),
            pl.BlockSpec((tm, SB_W), lambda m, n: (m, 0)),
            pl.BlockSpec((SB_W, tn), lambda m, n: (0, n)),
            pl.BlockSpec((tm, SWA_Q_W), lambda m, n: (m, 0)),
            pl.BlockSpec((SWA_Q_W, tn), lambda m, n: (0, n)),
        ],
        out_specs=pl.BlockSpec((tm, tn), lambda m, n: (m, n)),
        out_shape=jax.ShapeDtypeStruct((t, D_MODEL), BF16),
        compiler_params=_params("arbitrary", "arbitrary"),
        name="mix",
    )(xn, w_gates, w_gates, att_a, w_a, att_b, w_b)


def _oproj_kernel(x_ref, mixed_ref, w_ref, g_ref, h_ref, xn_ref):
    h = x_ref[...] + jnp.dot(mixed_ref[...], w_ref[...], preferred_element_type=F32)
    h_ref[...] = h
    xn_ref[...] = _rms(h, g_ref[...]).astype(BF16)


def _oproj(x2, mixed, w_o, g, tm=512):
    t = x2.shape[0]
    return pl.pallas_call(
        _oproj_kernel,
        grid=(t // tm,),
        in_specs=[
            pl.BlockSpec((tm, D_MODEL), lambda m: (m, 0)),
            pl.BlockSpec((tm, D_MODEL), lambda m: (m, 0)),
            _resident((D_MODEL, D_MODEL), lambda m: (0, 0)),
            _resident((1, D_MODEL), lambda m: (0, 0)),
        ],
        out_specs=[
            pl.BlockSpec((tm, D_MODEL), lambda m: (m, 0)),
            pl.BlockSpec((tm, D_MODEL), lambda m: (m, 0)),
        ],
        out_shape=[
            jax.ShapeDtypeStruct((t, D_MODEL), F32),
            jax.ShapeDtypeStruct((t, D_MODEL), BF16),
        ],
        compiler_params=_params("arbitrary"),
        name="oproj",
    )(x2, mixed, w_o, g)


CONV_HALO = 8
FFN_ROWS = 256


def _ffn_up_kernel(xn_ref, wg_ref, wu_ref, cg_ref, cu_ref, bg_ref, bu_ref, wd_ref, o_ref, wdb_ref, hg_ref, hu_ref,
                   wgb_ref, wub_ref, *, tiles_per_seq):
    m = pl.program_id(1)
    tm = xn_ref.shape[0]
    wdb_ref[...] = wd_ref[...].astype(BF16)
    for h_ref in (hg_ref, hu_ref):
        @pl.when(m % tiles_per_seq == 0)
        def _():
            h_ref[0:CONV_HALO, :] = jnp.zeros((CONV_HALO, h_ref.shape[1]), F32)

        @pl.when(m % tiles_per_seq != 0)
        def _():
            h_ref[0:CONV_HALO, :] = h_ref[tm:tm + CONV_HALO, :]

    def conv(h_ref, c_ref, b_ref, r0):
        h = h_ref[r0:r0 + CONV_HALO + FFN_ROWS, :]
        out = b_ref[...] + c_ref[CONV_WIDTH - 1:CONV_WIDTH, :] * h[CONV_HALO:]
        for back in range(1, CONV_WIDTH):
            shifted = pltpu.roll(h, back, axis=0)[CONV_HALO:]
            out = out + c_ref[CONV_WIDTH - 1 - back:CONV_WIDTH - back, :] * shifted
        return out

    @pl.when(m == 0)
    def _():
        wgb_ref[...] = wg_ref[...].astype(BF16)
        wub_ref[...] = wu_ref[...].astype(BF16)

    for r0 in range(0, tm, FFN_ROWS):
        xn = xn_ref[r0:r0 + FFN_ROWS, :]
        for h_ref, w_ref in ((hg_ref, wgb_ref), (hu_ref, wub_ref)):
            h_ref[CONV_HALO + r0:CONV_HALO + r0 + FFN_ROWS, :] = jnp.dot(
                xn, w_ref[...], preferred_element_type=F32)
        gate = conv(hg_ref, cg_ref, bg_ref, r0)
        up = conv(hu_ref, cu_ref, bu_ref, r0)
        o_ref[r0:r0 + FFN_ROWS, :] = (gate * jax.nn.sigmoid(gate) * up).astype(BF16)


def _ffn_up(xn2, w_up, conv_w, conv_b, w_down, seq, tm=1024, tn=512):
    t = xn2.shape[0]
    nn = D_FF // tn
    nm = t // tm
    job = _row_cast(w_down, nn * nm)
    kern = functools.partial(_ffn_up_kernel, tiles_per_seq=seq // tm)
    return pl.pallas_call(
        kern,
        grid=(nn, nm),
        in_specs=[
            pl.BlockSpec((tm, D_MODEL), lambda n, m: (m, 0)),
            pl.BlockSpec((D_MODEL, tn), lambda n, m: (0, n)),
            pl.BlockSpec((D_MODEL, tn), lambda n, m: (0, nn + n)),
            pl.BlockSpec((CONV_WIDTH, tn), lambda n, m: (0, n)),
            pl.BlockSpec((CONV_WIDTH, tn), lambda n, m: (0, nn + n)),
            pl.BlockSpec((1, tn), lambda n, m: (0, n)),
            pl.BlockSpec((1, tn), lambda n, m: (0, nn + n)),
            pl.BlockSpec(job.block, lambda n, m: job.src_index(n * nm + m)),
        ],
        out_specs=[
            pl.BlockSpec((tm, tn), lambda n, m: (m, n)),
            pl.BlockSpec(job.block, lambda n, m: job.out_index(n * nm + m)),
        ],
        out_shape=[jax.ShapeDtypeStruct((t, D_FF), BF16), jax.ShapeDtypeStruct(job.out_shape, BF16)],
        scratch_shapes=[
            pltpu.VMEM((CONV_HALO + tm, tn), F32),
            pltpu.VMEM((CONV_HALO + tm, tn), F32),
            pltpu.VMEM((D_MODEL, tn), BF16),
            pltpu.VMEM((D_MODEL, tn), BF16),
        ],
        compiler_params=_params("arbitrary", "arbitrary"),
        name="ffn_up",
    )(xn2, w_up, w_up, conv_w, conv_w, conv_b, conv_b, w_down)


def _ffn_down_kernel(h_ref, act_ref, w_ref, g_ref, o_ref):
    h = h_ref[...] + jnp.dot(act_ref[...], w_ref[...], preferred_element_type=F32)
    o_ref[...] = _rms(h, g_ref[...])


def _ffn_down(h1, act, w_down, g, tm=256):
    t = h1.shape[0]
    return pl.pallas_call(
        _ffn_down_kernel,
        grid=(t // tm,),
        in_specs=[
            pl.BlockSpec((tm, D_MODEL), lambda m: (m, 0)),
            pl.BlockSpec((tm, D_FF), lambda m: (m, 0)),
            _resident((D_FF, D_MODEL), lambda m: (0, 0)),
            _resident((1, D_MODEL), lambda m: (0, 0)),
        ],
        out_specs=pl.BlockSpec((tm, D_MODEL), lambda m: (m, 0)),
        out_shape=jax.ShapeDtypeStruct((t, D_MODEL), F32),
        compiler_params=_params("arbitrary"),
        name="ffn_down",
    )(h1, act, w_down, g)


def kernel(x, norm_mix_g, w_in, w_sb_out, w_swa_out, w_o, sinks, norm_ffn_g, w_up, conv_w, conv_b, w_down, norm_final_g):
    batch, seq, _ = x.shape
    assert w_in.shape[0] == 1, "single-layer block: the final norm is fused into the FFN down projection"
    row = lambda v: v.reshape(1, -1)
    x2 = x.reshape(batch * seq, D_MODEL)
    w_in2 = w_in.reshape(D_MODEL, -1)
    xn, qkv, vt = _inproj(x2, row(norm_mix_g[0]), w_in2[:, :QKV_W].astype(BF16), batch, seq)

    sb_steps = batch * (SB_W // (SB_GROUP * HEAD_DIM)) * (seq // SB_BLK)
    gate_rows, gate_cols = 512, 256
    col_blocks = 2 * D_MODEL // gate_cols
    assert QKV_W % gate_cols == 0 and (D_MODEL // gate_rows) * col_blocks == sb_steps
    gates_job = _CastJob(
        w_in2, (gate_rows, gate_cols),
        lambda s: (s // col_blocks, QKV_W // gate_cols + s % col_blocks),
        (D_MODEL, 2 * D_MODEL), lambda s: (s // col_blocks, s % col_blocks))
    jobs = [gates_job, _row_cast(w_sb_out.reshape(SB_W, D_MODEL), sb_steps),
            _row_cast(w_swa_out.reshape(SWA_Q_W, D_MODEL), sb_steps), _row_cast(w_o.reshape(D_MODEL, D_MODEL), sb_steps)]
    att_a, (w_gates, w_a, w_b, w_ob) = _sb_attention(qkv, vt, batch, seq, jobs)
    att_b = _swa_attention(qkv, sinks[0], batch, seq)
    mixed = _mix(xn, w_gates, att_a, w_a, att_b, w_b)
    h1, xn2 = _oproj(x2, mixed, w_ob, row(norm_ffn_g[0]))
    act, w_db = _ffn_up(xn2, w_up.reshape(D_MODEL, 2 * D_FF), conv_w[0], row(conv_b[0]), w_down.reshape(D_FF, D_MODEL), seq)
    out = _ffn_down(h1, act, w_db, row(norm_final_g))
    return out.reshape(batch, seq, D_MODEL)
```

```python
import functools
import math
from typing import Callable, NamedTuple

import jax
import jax.numpy as jnp
import numpy as np
from jax import lax
from jax.experimental import pallas as pl
from jax.experimental.pallas import tpu as pltpu

D_MODEL = 2048
HEAD_DIM = 64
SB_HEADS = 16
SWA_Q_HEADS = 16
SWA_KV_HEADS = 2
SWA_GROUP = SWA_Q_HEADS // SWA_KV_HEADS
WINDOW = 128
D_FF = 5632
CONV_WIDTH = 3
RMS_EPS = 1e-5
NEG_INF = -1e30

SB_W = SB_HEADS * HEAD_DIM
SWA_Q_W = SWA_Q_HEADS * HEAD_DIM
SWA_KV_W = SWA_KV_HEADS * HEAD_DIM
QKV_W = 3 * SB_W + SWA_Q_W + 2 * SWA_KV_W
LANES = 128
SUBLANES = 8

BF16 = jnp.bfloat16
F32 = jnp.float32

VMEM_LIMIT = 56 * 1024 * 1024


def _params(*sem):
    return pltpu.CompilerParams(dimension_semantics=sem, vmem_limit_bytes=VMEM_LIMIT)


def _resident(shape, index_map):
    return pl.BlockSpec(shape, index_map, pipeline_mode=pl.Buffered(1))


def _rms(xf, g):
    return xf * lax.rsqrt(jnp.mean(xf * xf, axis=-1, keepdims=True) + RMS_EPS) * g


SB_BLK = 256


def _inproj_kernel(x_ref, g_ref, w_ref, xn_ref, qkv_ref, vt_ref):
    xn = _rms(x_ref[...], g_ref[...]).astype(BF16)
    xn_ref[...] = xn
    qkv = jnp.dot(xn, w_ref[...], preferred_element_type=F32)
    qkv_ref[...] = qkv.astype(BF16)
    vt = qkv[:, 2 * SB_W:3 * SB_W].T.astype(BF16)
    for kb in range(vt_ref.shape[1]):
        vt_ref[0, kb] = vt[:, kb * SB_BLK:(kb + 1) * SB_BLK]


def _inproj(x2, g, w_qkv, batch, seq, tm=512):
    t = x2.shape[0]
    tiles_per_seq = seq // tm
    return pl.pallas_call(
        _inproj_kernel,
        grid=(t // tm,),
        in_specs=[
            pl.BlockSpec((tm, D_MODEL), lambda m: (m, 0)),
            _resident((1, D_MODEL), lambda m: (0, 0)),
            _resident((D_MODEL, QKV_W), lambda m: (0, 0)),
        ],
        out_specs=[
            pl.BlockSpec((tm, D_MODEL), lambda m: (m, 0)),
            pl.BlockSpec((tm, QKV_W), lambda m: (m, 0)),
            pl.BlockSpec((1, tm // SB_BLK, SB_W, SB_BLK), lambda m: (m // tiles_per_seq, m % tiles_per_seq, 0, 0)),
        ],
        out_shape=[
            jax.ShapeDtypeStruct((t, D_MODEL), BF16),
            jax.ShapeDtypeStruct((t, QKV_W), BF16),
            jax.ShapeDtypeStruct((batch, seq // SB_BLK, SB_W, SB_BLK), BF16),
        ],
        compiler_params=_params("arbitrary"),
        name="inproj",
    )(x2, g, w_qkv)


SB_GROUP = 8
SB_CHUNK = 256
SB_SUB = 128
HI_MASK = 0xFFFF0000


def _sb_kernel(q_ref, k_ref, vt_ref, *rest, blk, n_casts):
    cast_srcs, o_ref, cast_dsts = rest[:n_casts], rest[n_casts], rest[n_casts + 1:2 * n_casts + 1]
    za_ref, zb_ref, wa_ref, wb_ref, carry_ref, acc_ref = rest[2 * n_casts + 1:]
    for src, dst in zip(cast_srcs, cast_dsts):
        dst[...] = src[...].astype(BF16)
    i = pl.program_id(2)
    width = SB_GROUP * HEAD_DIM
    n = SB_GROUP * blk
    nsub = blk // SB_SUB
    q = q_ref[...] * BF16(1.0 / math.sqrt(HEAD_DIM))
    qt = q.astype(F32).T.astype(BF16)
    chunk_heads = SB_CHUNK // HEAD_DIM
    head_of_row = lax.broadcasted_iota(jnp.int32, (SB_CHUNK, blk), 0) // HEAD_DIM
    qstacks = []
    for cc in range(width // SB_CHUNK):
        qt_c = qt[cc * SB_CHUNK:(cc + 1) * SB_CHUNK]
        qstacks.append(jnp.concatenate(
            [jnp.where(head_of_row == c, qt_c, jnp.zeros_like(qt_c)) for c in range(chunk_heads)], axis=1))
    r = lax.broadcasted_iota(jnp.int32, (SB_SUB, SB_SUB), 0)
    c = lax.broadcasted_iota(jnp.int32, (SB_SUB, SB_SUB), 1)
    incl = (c >= r).astype(BF16)
    incl2 = jnp.concatenate([incl, incl], axis=1)
    key_idx = lax.broadcasted_iota(jnp.int32, (blk, n), 0)
    query_idx = lax.broadcasted_iota(jnp.int32, (blk, n), 1) & (blk - 1)
    causal = key_idx < query_idx
    last_sublane = lax.broadcasted_iota(jnp.int32, (SUBLANES, n), 0) == SUBLANES - 1

    def scores(j, z_ref):
        kb = k_ref[pl.ds(j * blk, blk), :]
        for cc, qs in enumerate(qstacks):
            z_ref[:, cc * chunk_heads * blk:(cc + 1) * chunk_heads * blk] = jnp.dot(
                kb[:, cc * SB_CHUNK:(cc + 1) * SB_CHUNK], qs, preferred_element_type=F32)

    def weights(z_ref, a_ref, masked):
        z = z_ref[...]
        carry = carry_ref[...]
        sp = jnp.where(z > 17.0, z, jnp.log(1.0 + jnp.exp(z)))
        if masked:
            sp = jnp.where(causal, sp, 0.0)
        ts = [None] * nsub
        for u in reversed(range(nsub)):
            lo_row, hi_row = u * SB_SUB, (u + 1) * SB_SUB
            tail = sp[hi_row - SUBLANES:hi_row] + jnp.where(last_sublane, carry, 0.0)
            spu = jnp.concatenate([sp[lo_row:hi_row - SUBLANES], tail], axis=0)
            hi = lax.bitcast_convert_type(lax.bitcast_convert_type(spu, jnp.uint32) & jnp.uint32(HI_MASK), F32)
            operand = jnp.concatenate([hi.astype(BF16), (spu - hi).astype(BF16)], axis=0)
            inc = jnp.dot(incl2, operand, preferred_element_type=F32)
            ts[u] = z[lo_row:hi_row] - inc
            carry = inc[0:1, :]
        a = jnp.exp(jnp.concatenate(ts, axis=0).astype(BF16))
        if masked:
            a = jnp.where(causal, a, jnp.zeros_like(a))
        a_ref[...] = a
        carry_ref[...] = carry

    def values(j, a_ref):
        for h in range(SB_GROUP):
            vt = vt_ref[0, j, h * HEAD_DIM:(h + 1) * HEAD_DIM, :]
            acc_ref[h * HEAD_DIM:(h + 1) * HEAD_DIM, :] += jnp.dot(
                vt, a_ref[:, h * blk:(h + 1) * blk], preferred_element_type=F32)

    carry_ref[...] = jnp.zeros_like(carry_ref)
    acc_ref[...] = jnp.zeros_like(acc_ref)
    first = i - 1 - i % 2

    @pl.when(i % 2 == 0)
    def _():
        scores(i, zb_ref)
        scores(jnp.maximum(first, 0), za_ref)
        weights(zb_ref, wb_ref, True)

    @pl.when(i % 2 == 1)
    def _():
        scores(i, za_ref)
        scores(i - 1, zb_ref)
        weights(za_ref, wa_ref, True)
        scores(jnp.maximum(first, 0), za_ref)
        values(i, wa_ref)
        weights(zb_ref, wb_ref, False)

    def pair(p, _):
        j = first - 2 * p
        scores(j - 1, zb_ref)
        values(j + 1, wb_ref)
        weights(za_ref, wa_ref, False)
        scores(jnp.maximum(j - 2, 0), za_ref)
        values(j, wa_ref)
        weights(zb_ref, wb_ref, False)
        return 0

    lax.fori_loop(0, i // 2, pair, 0)
    values(0, wb_ref)
    o_ref[...] = acc_ref[...].T.astype(BF16)


class _CastJob(NamedTuple):
    src: jax.Array
    block: tuple
    src_index: Callable
    out_shape: tuple
    out_index: Callable


def _row_cast(src, steps):
    rows, cols = src.shape
    assert rows % (steps * 16) == 0, (rows, steps)
    return _CastJob(src, (rows // steps, cols), lambda s: (s, 0), (rows, cols), lambda s: (s, 0))


def _sb_attention(qkv, vt, batch, seq, cast_jobs, blk=SB_BLK):
    nq = seq // blk
    width = SB_GROUP * HEAD_DIM
    groups = SB_W // width
    step = lambda b, g, i: (b * groups + g) * nq + i
    at_step = lambda index: (lambda b, g, i: index(step(b, g, i)))
    outs = pl.pallas_call(
        functools.partial(_sb_kernel, blk=blk, n_casts=len(cast_jobs)),
        grid=(batch, groups, nq),
        in_specs=[
            pl.BlockSpec((blk, width), lambda b, g, i: (b * nq + i, g)),
            pl.BlockSpec((seq, width), lambda b, g, i: (b, groups + g)),
            pl.BlockSpec((1, nq, width, blk), lambda b, g, i: (b, 0, g, 0)),
        ] + [pl.BlockSpec(job.block, at_step(job.src_index)) for job in cast_jobs],
        out_specs=[pl.BlockSpec((blk, width), lambda b, g, i: (b * nq + i, g))]
        + [pl.BlockSpec(job.block, at_step(job.out_index)) for job in cast_jobs],
        out_shape=[jax.ShapeDtypeStruct((batch * seq, SB_W), BF16)]
        + [jax.ShapeDtypeStruct(job.out_shape, BF16) for job in cast_jobs],
        scratch_shapes=[
            pltpu.VMEM((blk, SB_GROUP * blk), F32),
            pltpu.VMEM((blk, SB_GROUP * blk), F32),
            pltpu.VMEM((blk, SB_GROUP * blk), BF16),
            pltpu.VMEM((blk, SB_GROUP * blk), BF16),
            pltpu.VMEM((1, SB_GROUP * blk), F32),
            pltpu.VMEM((width, blk), F32),
        ],
        compiler_params=_params("arbitrary", "arbitrary", "arbitrary"),
        name="sb_attn",
    )(qkv, qkv, vt, *[job.src for job in cast_jobs])
    return outs[0], outs[1:]


def _swa_kernel(sink_ref, q_ref, kp_ref, kc_ref, vp_ref, vc_ref, o_ref, bias_ref, *, slopes):
    blk = q_ref.shape[0]
    nk = 2 * blk
    first_step = (pl.program_id(0) == 0) & (pl.program_id(1) == 0)

    @pl.when(first_step)
    def _():
        key = lax.broadcasted_iota(jnp.int32, (nk, blk), 0)
        qry = lax.broadcasted_iota(jnp.int32, (nk, blk), 1)
        dist = qry - key + blk
        in_window = (dist >= 0) & (dist < WINDOW)
        distf = dist.astype(F32)
        for h in range(SWA_Q_HEADS):
            general = jnp.where(in_window, -(slopes[h] * distf), NEG_INF)
            bias_ref[1, :, h * blk:(h + 1) * blk] = general
            bias_ref[0, :, h * blk:(h + 1) * blk] = jnp.where(key >= blk, general, NEG_INF)

    table = jnp.minimum(pl.program_id(1), 1)
    q = q_ref[...] * BF16(1.0 / math.sqrt(HEAD_DIM))
    qt = q.astype(F32).T.astype(BF16)
    zeros = jnp.zeros((HEAD_DIM, blk), BF16)
    cols = []
    for h in range(SWA_Q_HEADS):
        qh = qt[h * HEAD_DIM:(h + 1) * HEAD_DIM]
        cols.append(jnp.concatenate([qh, zeros] if h // SWA_GROUP == 0 else [zeros, qh], axis=0))
    qstack = jnp.concatenate(cols, axis=1)
    kb = jnp.concatenate([kp_ref[...], kc_ref[...]], axis=0)
    z = jnp.dot(kb, qstack, preferred_element_type=F32) + bias_ref[table]
    sink_row = jnp.concatenate([jnp.full((1, blk), sink_ref[h], F32) for h in range(SWA_Q_HEADS)], axis=1)
    zmax = jnp.maximum(jnp.max(z, axis=0, keepdims=True), sink_row)
    p = jnp.exp((z - zmax).astype(BF16))
    sink_term = jnp.exp(sink_row - zmax)
    vt = jnp.concatenate([vp_ref[...], vc_ref[...]], axis=0).astype(F32).T
    ones = jnp.ones((HEAD_DIM, nk), F32)
    gw = SWA_GROUP * blk
    outs = []
    for g in range(SWA_KV_HEADS):
        vext = jnp.concatenate([vt[g * HEAD_DIM:(g + 1) * HEAD_DIM], ones], axis=0).astype(BF16)
        ot = jnp.dot(vext, p[:, g * gw:(g + 1) * gw], preferred_element_type=F32)
        denom = ot[HEAD_DIM:HEAD_DIM + 1, :] + sink_term[:, g * gw:(g + 1) * gw]
        res = ot[:HEAD_DIM] / denom
        outs += [res[:, k * blk:(k + 1) * blk] for k in range(SWA_GROUP)]
    o_ref[...] = jnp.concatenate(outs, axis=0).T.astype(BF16)


def _swa_attention(qkv, sinks, batch, seq, blk=128):
    nb = seq // blk
    qcol = 3 * SB_W // SWA_Q_W
    kcol = (3 * SB_W + SWA_Q_W) // SWA_KV_W
    vcol = kcol + 1
    slopes = [float(v) for v in np.power(2.0, -8.0 * np.arange(1, SWA_Q_HEADS + 1) / SWA_Q_HEADS).astype(np.float32)]
    cur = lambda col: (lambda b, n: (b * nb + n, col))
    prev = lambda col: (lambda b, n: (b * nb + jnp.maximum(n - 1, 0), col))
    return pl.pallas_call(
        functools.partial(_swa_kernel, slopes=slopes),
        grid=(batch, nb),
        in_specs=[
            pl.BlockSpec(memory_space=pltpu.SMEM),
            pl.BlockSpec((blk, SWA_Q_W), cur(qcol)),
            pl.BlockSpec((blk, SWA_KV_W), prev(kcol)),
            pl.BlockSpec((blk, SWA_KV_W), cur(kcol)),
            pl.BlockSpec((blk, SWA_KV_W), prev(vcol)),
            pl.BlockSpec((blk, SWA_KV_W), cur(vcol)),
        ],
        out_specs=pl.BlockSpec((blk, SWA_Q_W), lambda b, n: (b * nb + n, 0)),
        out_shape=jax.ShapeDtypeStruct((batch * seq, SWA_Q_W), BF16),
        scratch_shapes=[pltpu.VMEM((2, 2 * blk, SWA_Q_HEADS * blk), F32)],
        compiler_params=_params("arbitrary", "arbitrary"),
        name="swa_attn",
    )(sinks, qkv, qkv, qkv, qkv, qkv)


def _mix_kernel(xn_ref, wga_ref, wgb_ref, a_ref, wa_ref, b_ref, wb_ref, o_ref):
    xn = xn_ref[...]
    ga = jnp.dot(xn, wga_ref[...], preferred_element_type=F32)
    gb = jnp.dot(xn, wgb_ref[...], preferred_element_type=F32)
    ya = jnp.dot(a_ref[...], wa_ref[...], preferred_element_type=F32)
    yb = jnp.dot(b_ref[...], wb_ref[...], preferred_element_type=F32)
    o_ref[...] = (jax.nn.sigmoid(ga) * ya + jax.nn.sigmoid(gb) * yb).astype(BF16)


def _mix(xn, w_gates, att_a, w_a, att_b, w_b, tm=1024, tn=512):
    t = xn.shape[0]
    nn = D_MODEL // tn
    return pl.pallas_call(
        _mix_kernel,
        grid=(t // tm, nn),
        in_specs=[
            pl.BlockSpec((tm, D_MODEL), lambda m, n: (m, 0)),
            pl.BlockSpec((D_MODEL, tn), lambda m, n: (0, n)),
            pl.BlockSpec((D_MODEL, tn), lambda m, n: (0, nn + n)),
            pl.BlockSpec((tm, SB_W), lambda m, n: (m, 0)),
            pl.BlockSpec((SB_W, tn), lambda m, n: (0, n)),
            pl.BlockSpec((tm, SWA_Q_W), lambda m, n: (m, 0)),
            pl.BlockSpec((SWA_Q_W, tn), lambda m, n: (0, n)),
        ],
        out_specs=pl.BlockSpec((tm, tn), lambda m, n: (m, n)),
        out_shape=jax.ShapeDtypeStruct((t, D_MODEL), BF16),
        compiler_params=_params("arbitrary", "arbitrary"),
        name="mix",
    )(xn, w_gates, w_gates, att_a, w_a, att_b, w_b)


def _oproj_kernel(x_ref, mixed_ref, w_ref, g_ref, h_ref, xn_ref):
    h = x_ref[...] + jnp.dot(mixed_ref[...], w_ref[...], preferred_element_type=F32)
    h_ref[...] = h
    xn_ref[...] = _rms(h, g_ref[...]).astype(BF16)


def _oproj(x2, mixed, w_o, g, tm=512):
    t = x2.shape[0]
    return pl.pallas_call(
        _oproj_kernel,
        grid=(t // tm,),
        in_specs=[
            pl.BlockSpec((tm, D_MODEL), lambda m: (m, 0)),
            pl.BlockSpec((tm, D_MODEL), lambda m: (m, 0)),
            _resident((D_MODEL, D_MODEL), lambda m: (0, 0)),
            _resident((1, D_MODEL), lambda m: (0, 0)),
        ],
        out_specs=[
            pl.BlockSpec((tm, D_MODEL), lambda m: (m, 0)),
            pl.BlockSpec((tm, D_MODEL), lambda m: (m, 0)),
        ],
        out_shape=[
            jax.ShapeDtypeStruct((t, D_MODEL), F32),
            jax.ShapeDtypeStruct((t, D_MODEL), BF16),
        ],
        compiler_params=_params("arbitrary"),
        name="oproj",
    )(x2, mixed, w_o, g)


CONV_HALO = 8
FFN_ROWS = 256


def _ffn_up_kernel(xn_ref, wg_ref, wu_ref, cg_ref, cu_ref, bg_ref, bu_ref, wd_ref, o_ref, wdb_ref, hg_ref, hu_ref,
                   wgb_ref, wub_ref, *, tiles_per_seq):
    m = pl.program_id(1)
    tm = xn_ref.shape[0]
    wdb_ref[...] = wd_ref[...].astype(BF16)
    for h_ref in (hg_ref, hu_ref):
        @pl.when(m % tiles_per_seq == 0)
        def _():
            h_ref[0:CONV_HALO, :] = jnp.zeros((CONV_HALO, h_ref.shape[1]), F32)

        @pl.when(m % tiles_per_seq != 0)
        def _():
            h_ref[0:CONV_HALO, :] = h_ref[tm:tm + CONV_HALO, :]

    def conv(h_ref, c_ref, b_ref, r0):
        h = h_ref[r0:r0 + CONV_HALO + FFN_ROWS, :]
        out = b_ref[...] + c_ref[CONV_WIDTH - 1:CONV_WIDTH, :] * h[CONV_HALO:]
        for back in range(1, CONV_WIDTH):
            shifted = pltpu.roll(h, back, axis=0)[CONV_HALO:]
            out = out + c_ref[CONV_WIDTH - 1 - back:CONV_WIDTH - back, :] * shifted
        return out

    @pl.when(m == 0)
    def _():
        wgb_ref[...] = wg_ref[...].astype(BF16)
        wub_ref[...] = wu_ref[...].astype(BF16)

    for r0 in range(0, tm, FFN_ROWS):
        xn = xn_ref[r0:r0 + FFN_ROWS, :]
        for h_ref, w_ref in ((hg_ref, wgb_ref), (hu_ref, wub_ref)):
            h_ref[CONV_HALO + r0:CONV_HALO + r0 + FFN_ROWS, :] = jnp.dot(
                xn, w_ref[...], preferred_element_type=F32)
        gate = conv(hg_ref, cg_ref, bg_ref, r0)
        up = conv(hu_ref, cu_ref, bu_ref, r0)
        o_ref[r0:r0 + FFN_ROWS, :] = (gate * jax.nn.sigmoid(gate) * up).astype(BF16)


def _ffn_up(xn2, w_up, conv_w, conv_b, w_down, seq, tm=2048, tn=512):
    t = xn2.shape[0]
    nn = D_FF // tn
    nm = t // tm
    job = _row_cast(w_down, nn * nm)
    kern = functools.partial(_ffn_up_kernel, tiles_per_seq=seq // tm)
    return pl.pallas_call(
        kern,
        grid=(nn, nm),
        in_specs=[
            pl.BlockSpec((tm, D_MODEL), lambda n, m: (m, 0)),
            pl.BlockSpec((D_MODEL, tn), lambda n, m: (0, n)),
            pl.BlockSpec((D_MODEL, tn), lambda n, m: (0, nn + n)),
            pl.BlockSpec((CONV_WIDTH, tn), lambda n, m: (0, n)),
            pl.BlockSpec((CONV_WIDTH, tn), lambda n, m: (0, nn + n)),
            pl.BlockSpec((1, tn), lambda n, m: (0, n)),
            pl.BlockSpec((1, tn), lambda n, m: (0, nn + n)),
            pl.BlockSpec(job.block, lambda n, m: job.src_index(n * nm + m)),
        ],
        out_specs=[
            pl.BlockSpec((tm, tn), lambda n, m: (m, n)),
            pl.BlockSpec(job.block, lambda n, m: job.out_index(n * nm + m)),
        ],
        out_shape=[jax.ShapeDtypeStruct((t, D_FF), BF16), jax.ShapeDtypeStruct(job.out_shape, BF16)],
        scratch_shapes=[
            pltpu.VMEM((CONV_HALO + tm, tn), F32),
            pltpu.VMEM((CONV_HALO + tm, tn), F32),
            pltpu.VMEM((D_MODEL, tn), BF16),
            pltpu.VMEM((D_MODEL, tn), BF16),
        ],
        compiler_params=_params("arbitrary", "arbitrary"),
        name="ffn_up",
    )(xn2, w_up, w_up, conv_w, conv_w, conv_b, conv_b, w_down)


def _ffn_down_kernel(h_ref, act_ref, w_ref, g_ref, o_ref):
    h = h_ref[...] + jnp.dot(act_ref[...], w_ref[...], preferred_element_type=F32)
    o_ref[...] = _rms(h, g_ref[...])


def _ffn_down(h1, act, w_down, g, tm=256):
    t = h1.shape[0]
    return pl.pallas_call(
        _ffn_down_kernel,
        grid=(t // tm,),
        in_specs=[
            pl.BlockSpec((tm, D_MODEL), lambda m: (m, 0)),
            pl.BlockSpec((tm, D_FF), lambda m: (m, 0)),
            _resident((D_FF, D_MODEL), lambda m: (0, 0)),
            _resident((1, D_MODEL), lambda m: (0, 0)),
        ],
        out_specs=pl.BlockSpec((tm, D_MODEL), lambda m: (m, 0)),
        out_shape=jax.ShapeDtypeStruct((t, D_MODEL), F32),
        compiler_params=_params("arbitrary"),
        name="ffn_down",
    )(h1, act, w_down, g)


def kernel(x, norm_mix_g, w_in, w_sb_out, w_swa_out, w_o, sinks, norm_ffn_g, w_up, conv_w, conv_b, w_down, norm_final_g):
    batch, seq, _ = x.shape
    assert w_in.shape[0] == 1, "single-layer block: the final norm is fused into the FFN down projection"
    row = lambda v: v.reshape(1, -1)
    x2 = x.reshape(batch * seq, D_MODEL)
    w_in2 = w_in.reshape(D_MODEL, -1)
    xn, qkv, vt = _inproj(x2, row(norm_mix_g[0]), w_in2[:, :QKV_W].astype(BF16), batch, seq)

    sb_steps = batch * (SB_W // (SB_GROUP * HEAD_DIM)) * (seq // SB_BLK)
    gate_rows, gate_cols = 512, 256
    col_blocks = 2 * D_MODEL // gate_cols
    assert QKV_W % gate_cols == 0 and (D_MODEL // gate_rows) * col_blocks == sb_steps
    gates_job = _CastJob(
        w_in2, (gate_rows, gate_cols),
        lambda s: (s // col_blocks, QKV_W // gate_cols + s % col_blocks),
        (D_MODEL, 2 * D_MODEL), lambda s: (s // col_blocks, s % col_blocks))
    jobs = [gates_job, _row_cast(w_sb_out.reshape(SB_W, D_MODEL), sb_steps),
            _row_cast(w_swa_out.reshape(SWA_Q_W, D_MODEL), sb_steps), _row_cast(w_o.reshape(D_MODEL, D_MODEL), sb_steps)]
    att_a, (w_gates, w_a, w_b, w_ob) = _sb_attention(qkv, vt, batch, seq, jobs)
    att_b = _swa_attention(qkv, sinks[0], batch, seq)
    mixed = _mix(xn, w_gates, att_a, w_a, att_b, w_b)
    h1, xn2 = _oproj(x2, mixed, w_ob, row(norm_ffn_g[0]))
    act, w_db = _ffn_up(xn2, w_up.reshape(D_MODEL, 2 * D_FF), conv_w[0], row(conv_b[0]), w_down.reshape(D_FF, D_MODEL), seq)
    out = _ffn_down(h1, act, w_db, row(norm_final_g))
    return out.reshape(batch, seq, D_MODEL)
```

```python
import functools
import math
from typing import Callable, NamedTuple

import jax
import jax.numpy as jnp
import numpy as np
from jax import lax
from jax.experimental import pallas as pl
from jax.experimental.pallas import tpu as pltpu

D_MODEL = 2048
HEAD_DIM = 64
SB_HEADS = 16
SWA_Q_HEADS = 16
SWA_KV_HEADS = 2
SWA_GROUP = SWA_Q_HEADS // SWA_KV_HEADS
WINDOW = 128
D_FF = 5632
CONV_WIDTH = 3
RMS_EPS = 1e-5
NEG_INF = -1e30

SB_W = SB_HEADS * HEAD_DIM
SWA_Q_W = SWA_Q_HEADS * HEAD_DIM
SWA_KV_W = SWA_KV_HEADS * HEAD_DIM
QKV_W = 3 * SB_W + SWA_Q_W + 2 * SWA_KV_W
LANES = 128

BF16 = jnp.bfloat16
F32 = jnp.float32

VMEM_LIMIT = 56 * 1024 * 1024


def _params(*sem):
    return pltpu.CompilerParams(dimension_semantics=sem, vmem_limit_bytes=VMEM_LIMIT)


def _resident(shape, index_map):
    return pl.BlockSpec(shape, index_map, pipeline_mode=pl.Buffered(1))


def _rms(xf, g):
    return xf * lax.rsqrt(jnp.mean(xf * xf, axis=-1, keepdims=True) + RMS_EPS) * g


SB_BLK = 256


def _inproj_kernel(x_ref, g_ref, w_ref, xn_ref, qkv_ref, vt_ref):
    xn = _rms(x_ref[...], g_ref[...]).astype(BF16)
    xn_ref[...] = xn
    qkv = jnp.dot(xn, w_ref[...], preferred_element_type=F32)
    qkv_ref[...] = qkv.astype(BF16)
    vt = qkv[:, 2 * SB_W:3 * SB_W].T.astype(BF16)
    for kb in range(vt_ref.shape[1]):
        vt_ref[0, kb] = vt[:, kb * SB_BLK:(kb + 1) * SB_BLK]


def _inproj(x2, g, w_qkv, batch, seq, tm=512):
    t = x2.shape[0]
    tiles_per_seq = seq // tm
    return pl.pallas_call(
        _inproj_kernel,
        grid=(t // tm,),
        in_specs=[
            pl.BlockSpec((tm, D_MODEL), lambda m: (m, 0)),
            _resident((1, D_MODEL), lambda m: (0, 0)),
            _resident((D_MODEL, QKV_W), lambda m: (0, 0)),
        ],
        out_specs=[
            pl.BlockSpec((tm, D_MODEL), lambda m: (m, 0)),
            pl.BlockSpec((tm, QKV_W), lambda m: (m, 0)),
            pl.BlockSpec((1, tm // SB_BLK, SB_W, SB_BLK), lambda m: (m // tiles_per_seq, m % tiles_per_seq, 0, 0)),
        ],
        out_shape=[
            jax.ShapeDtypeStruct((t, D_MODEL), BF16),
            jax.ShapeDtypeStruct((t, QKV_W), BF16),
            jax.ShapeDtypeStruct((batch, seq // SB_BLK, SB_W, SB_BLK), BF16),
        ],
        compiler_params=_params("arbitrary"),
        name="inproj",
    )(x2, g, w_qkv)


SB_GROUP = 8
SB_CHUNK = 256


def _sb_kernel(q_ref, k_ref, vt_ref, *rest, blk, n_casts):
    cast_srcs, o_ref, cast_dsts = rest[:n_casts], rest[n_casts], rest[n_casts + 1:2 * n_casts + 1]
    za_ref, zb_ref, wa_ref, wb_ref, carry_ref, acc_ref = rest[2 * n_casts + 1:]
    for src, dst in zip(cast_srcs, cast_dsts):
        dst[...] = src[...].astype(BF16)
    i = pl.program_id(2)
    width = SB_GROUP * HEAD_DIM
    n = SB_GROUP * blk
    q = q_ref[...] * BF16(1.0 / math.sqrt(HEAD_DIM))
    qt = q.astype(F32).T.astype(BF16)
    chunk_heads = SB_CHUNK // HEAD_DIM
    head_of_row = lax.broadcasted_iota(jnp.int32, (SB_CHUNK, blk), 0) // HEAD_DIM
    qstacks = []
    for cc in range(width // SB_CHUNK):
        qt_c = qt[cc * SB_CHUNK:(cc + 1) * SB_CHUNK]
        qstacks.append(jnp.concatenate(
            [jnp.where(head_of_row == c, qt_c, jnp.zeros_like(qt_c)) for c in range(chunk_heads)], axis=1))
    r = lax.broadcasted_iota(jnp.int32, (blk, blk), 0)
    c = lax.broadcasted_iota(jnp.int32, (blk, blk), 1)
    incl = (c >= r).astype(BF16)
    key_idx = lax.broadcasted_iota(jnp.int32, (blk, n), 0)
    query_idx = lax.broadcasted_iota(jnp.int32, (blk, n), 1) & (blk - 1)
    causal = key_idx < query_idx

    def scores(j, z_ref):
        kb = k_ref[pl.ds(j * blk, blk), :]
        for cc, qs in enumerate(qstacks):
            z_ref[:, cc * chunk_heads * blk:(cc + 1) * chunk_heads * blk] = jnp.dot(
                kb[:, cc * SB_CHUNK:(cc + 1) * SB_CHUNK], qs, preferred_element_type=F32)

    def weights(z_ref, a_ref, masked):
        z = z_ref[...]
        carry = carry_ref[...]
        sp = jnp.where(z > 17.0, z, jnp.log(1.0 + jnp.exp(z)))
        if masked:
            sp = jnp.where(causal, sp, 0.0)
        inc = jnp.dot(incl, sp.astype(BF16), preferred_element_type=F32)
        a = jnp.exp((z - inc - carry).astype(BF16))
        if masked:
            a = jnp.where(causal, a, jnp.zeros_like(a))
        a_ref[...] = a
        carry_ref[...] = carry + inc[0:1, :]

    def values(j, a_ref):
        for h in range(SB_GROUP):
            vt = vt_ref[0, j, h * HEAD_DIM:(h + 1) * HEAD_DIM, :]
            acc_ref[h * HEAD_DIM:(h + 1) * HEAD_DIM, :] += jnp.dot(
                vt, a_ref[:, h * blk:(h + 1) * blk], preferred_element_type=F32)

    carry_ref[...] = jnp.zeros_like(carry_ref)
    acc_ref[...] = jnp.zeros_like(acc_ref)
    first = i - 1 - i % 2

    @pl.when(i % 2 == 0)
    def _():
        scores(i, zb_ref)
        scores(jnp.maximum(first, 0), za_ref)
        weights(zb_ref, wb_ref, True)

    @pl.when(i % 2 == 1)
    def _():
        scores(i, za_ref)
        scores(i - 1, zb_ref)
        weights(za_ref, wa_ref, True)
        scores(jnp.maximum(first, 0), za_ref)
        values(i, wa_ref)
        weights(zb_ref, wb_ref, False)

    def pair(p, _):
        j = first - 2 * p
        scores(j - 1, zb_ref)
        values(j + 1, wb_ref)
        weights(za_ref, wa_ref, False)
        scores(jnp.maximum(j - 2, 0), za_ref)
        values(j, wa_ref)
        weights(zb_ref, wb_ref, False)
        return 0

    lax.fori_loop(0, i // 2, pair, 0)
    values(0, wb_ref)
    o_ref[...] = acc_ref[...].T.astype(BF16)


class _CastJob(NamedTuple):
    src: jax.Array
    block: tuple
    src_index: Callable
    out_shape: tuple
    out_index: Callable


def _row_cast(src, steps):
    rows, cols = src.shape
    assert rows % (steps * 16) == 0, (rows, steps)
    return _CastJob(src, (rows // steps, cols), lambda s: (s, 0), (rows, cols), lambda s: (s, 0))


def _sb_attention(qkv, vt, batch, seq, cast_jobs, blk=SB_BLK):
    nq = seq // blk
    width = SB_GROUP * HEAD_DIM
    groups = SB_W // width
    step = lambda b, g, i: (b * groups + g) * nq + i
    at_step = lambda index: (lambda b, g, i: index(step(b, g, i)))
    outs = pl.pallas_call(
        functools.partial(_sb_kernel, blk=blk, n_casts=len(cast_jobs)),
        grid=(batch, groups, nq),
        in_specs=[
            pl.BlockSpec((blk, width), lambda b, g, i: (b * nq + i, g)),
            pl.BlockSpec((seq, width), lambda b, g, i: (b, groups + g)),
            pl.BlockSpec((1, nq, width, blk), lambda b, g, i: (b, 0, g, 0)),
        ] + [pl.BlockSpec(job.block, at_step(job.src_index)) for job in cast_jobs],
        out_specs=[pl.BlockSpec((blk, width), lambda b, g, i: (b * nq + i, g))]
        + [pl.BlockSpec(job.block, at_step(job.out_index)) for job in cast_jobs],
        out_shape=[jax.ShapeDtypeStruct((batch * seq, SB_W), BF16)]
        + [jax.ShapeDtypeStruct(job.out_shape, BF16) for job in cast_jobs],
        scratch_shapes=[
            pltpu.VMEM((blk, SB_GROUP * blk), F32),
            pltpu.VMEM((blk, SB_GROUP * blk), F32),
            pltpu.VMEM((blk, SB_GROUP * blk), BF16),
            pltpu.VMEM((blk, SB_GROUP * blk), BF16),
            pltpu.VMEM((1, SB_GROUP * blk), F32),
            pltpu.VMEM((width, blk), F32),
        ],
        compiler_params=_params("arbitrary", "arbitrary", "arbitrary"),
        name="sb_attn",
    )(qkv, qkv, vt, *[job.src for job in cast_jobs])
    return outs[0], outs[1:]


def _swa_kernel(sink_ref, q_ref, kp_ref, kc_ref, vp_ref, vc_ref, o_ref, bias_ref, *, slopes):
    blk = q_ref.shape[0]
    nk = 2 * blk
    first_step = (pl.program_id(0) == 0) & (pl.program_id(1) == 0)

    @pl.when(first_step)
    def _():
        key = lax.broadcasted_iota(jnp.int32, (nk, blk), 0)
        qry = lax.broadcasted_iota(jnp.int32, (nk, blk), 1)
        dist = qry - key + blk
        in_window = (dist >= 0) & (dist < WINDOW)
        distf = dist.astype(F32)
        for h in range(SWA_Q_HEADS):
            general = jnp.where(in_window, -(slopes[h] * distf), NEG_INF)
            bias_ref[1, :, h * blk:(h + 1) * blk] = general
            bias_ref[0, :, h * blk:(h + 1) * blk] = jnp.where(key >= blk, general, NEG_INF)

    table = jnp.minimum(pl.program_id(1), 1)
    q = q_ref[...] * BF16(1.0 / math.sqrt(HEAD_DIM))
    qt = q.astype(F32).T.astype(BF16)
    zeros = jnp.zeros((HEAD_DIM, blk), BF16)
    cols = []
    for h in range(SWA_Q_HEADS):
        qh = qt[h * HEAD_DIM:(h + 1) * HEAD_DIM]
        cols.append(jnp.concatenate([qh, zeros] if h // SWA_GROUP == 0 else [zeros, qh], axis=0))
    qstack = jnp.concatenate(cols, axis=1)
    kb = jnp.concatenate([kp_ref[...], kc_ref[...]], axis=0)
    z = jnp.dot(kb, qstack, preferred_element_type=F32) + bias_ref[table]
    sink_row = jnp.concatenate([jnp.full((1, blk), sink_ref[h], F32) for h in range(SWA_Q_HEADS)], axis=1)
    zmax = jnp.maximum(jnp.max(z, axis=0, keepdims=True), sink_row)
    p = jnp.exp((z - zmax).astype(BF16))
    sink_term = jnp.exp(sink_row - zmax)
    vt = jnp.concatenate([vp_ref[...], vc_ref[...]], axis=0).astype(F32).T
    ones = jnp.ones((HEAD_DIM, nk), F32)
    gw = SWA_GROUP * blk
    outs = []
    for g in range(SWA_KV_HEADS):
        vext = jnp.concatenate([vt[g * HEAD_DIM:(g + 1) * HEAD_DIM], ones], axis=0).astype(BF16)
        ot = jnp.dot(vext, p[:, g * gw:(g + 1) * gw], preferred_element_type=F32)
        denom = ot[HEAD_DIM:HEAD_DIM + 1, :] + sink_term[:, g * gw:(g + 1) * gw]
        res = ot[:HEAD_DIM] / denom
        outs += [res[:, k * blk:(k + 1) * blk] for k in range(SWA_GROUP)]
    o_ref[...] = jnp.concatenate(outs, axis=0).T.astype(BF16)


def _swa_attention(qkv, sinks, batch, seq, blk=128):
    nb = seq // blk
    qcol = 3 * SB_W // SWA_Q_W
    kcol = (3 * SB_W + SWA_Q_W) // SWA_KV_W
    vcol = kcol + 1
    slopes = [float(v) for v in np.power(2.0, -8.0 * np.arange(1, SWA_Q_HEADS + 1) / SWA_Q_HEADS).astype(np.float32)]
    cur = lambda col: (lambda b, n: (b * nb + n, col))
    prev = lambda col: (lambda b, n: (b * nb + jnp.maximum(n - 1, 0), col))
    return pl.pallas_call(
        functools.partial(_swa_kernel, slopes=slopes),
        grid=(batch, nb),
        in_specs=[
            pl.BlockSpec(memory_space=pltpu.SMEM),
            pl.BlockSpec((blk, SWA_Q_W), cur(qcol)),
            pl.BlockSpec((blk, SWA_KV_W), prev(kcol)),
            pl.BlockSpec((blk, SWA_KV_W), cur(kcol)),
            pl.BlockSpec((blk, SWA_KV_W), prev(vcol)),
            pl.BlockSpec((blk, SWA_KV_W), cur(vcol)),
        ],
        out_specs=pl.BlockSpec((blk, SWA_Q_W), lambda b, n: (b * nb + n, 0)),
        out_shape=jax.ShapeDtypeStruct((batch * seq, SWA_Q_W), BF16),
        scratch_shapes=[pltpu.VMEM((2, 2 * blk, SWA_Q_HEADS * blk), F32)],
        compiler_params=_params("arbitrary", "arbitrary"),
        name="swa_attn",
    )(sinks, qkv, qkv, qkv, qkv, qkv)


def _mix_kernel(xn_ref, wga_ref, wgb_ref, a_ref, wa_ref, b_ref, wb_ref, o_ref):
    xn = xn_ref[...]
    ga = jnp.dot(xn, wga_ref[...], preferred_element_type=F32)
    gb = jnp.dot(xn, wgb_ref[...], preferred_element_type=F32)
    ya = jnp.dot(a_ref[...], wa_ref[...], preferred_element_type=F32)
    yb = jnp.dot(b_ref[...], wb_ref[...], preferred_element_type=F32)
    o_ref[...] = (jax.nn.sigmoid(ga) * ya + jax.nn.sigmoid(gb) * yb).astype(BF16)


def _mix(xn, w_gates, att_a, w_a, att_b, w_b, tm=1024, tn=1024):
    t = xn.shape[0]
    nn = D_MODEL // tn
    return pl.pallas_call(
        _mix_kernel,
        grid=(t // tm, nn),
        in_specs=[
            pl.BlockSpec((tm, D_MODEL), lambda m, n: (m, 0)),
            pl.BlockSpec((D_MODEL, tn), lambda m, n: (0, n)),
            pl.BlockSpec((D_MODEL, tn), lambda m, n: (0, nn + n)),
            pl.BlockSpec((tm, SB_W), lambda m, n: (m, 0)),
            pl.BlockSpec((SB_W, tn), lambda m, n: (0, n)),
            pl.BlockSpec((tm, SWA_Q_W), lambda m, n: (m, 0)),
            pl.BlockSpec((SWA_Q_W, tn), lambda m, n: (0, n)),
        ],
        out_specs=pl.BlockSpec((tm, tn), lambda m, n: (m, n)),
        out_shape=jax.ShapeDtypeStruct((t, D_MODEL), BF16),
        compiler_params=_params("arbitrary", "arbitrary"),
        name="mix",
    )(xn, w_gates, w_gates, att_a, w_a, att_b, w_b)


def _oproj_kernel(x_ref, mixed_ref, w_ref, g_ref, h_ref, xn_ref):
    h = x_ref[...] + jnp.dot(mixed_ref[...], w_ref[...], preferred_element_type=F32)
    h_ref[...] = h
    xn_ref[...] = _rms(h, g_ref[...]).astype(BF16)


def _oproj(x2, mixed, w_o, g, tm=512):
    t = x2.shape[0]
    return pl.pallas_call(
        _oproj_kernel,
        grid=(t // tm,),
        in_specs=[
            pl.BlockSpec((tm, D_MODEL), lambda m: (m, 0)),
            pl.BlockSpec((tm, D_MODEL), lambda m: (m, 0)),
            _resident((D_MODEL, D_MODEL), lambda m: (0, 0)),
            _resident((1, D_MODEL), lambda m: (0, 0)),
        ],
        out_specs=[
            pl.BlockSpec((tm, D_MODEL), lambda m: (m, 0)),
            pl.BlockSpec((tm, D_MODEL), lambda m: (m, 0)),
        ],
        out_shape=[
            jax.ShapeDtypeStruct((t, D_MODEL), F32),
            jax.ShapeDtypeStruct((t, D_MODEL), BF16),
        ],
        compiler_params=_params("arbitrary"),
        name="oproj",
    )(x2, mixed, w_o, g)


CONV_HALO = 8
FFN_ROWS = 256


def _ffn_up_kernel(xn_ref, wg_ref, wu_ref, cg_ref, cu_ref, bg_ref, bu_ref, wd_ref, o_ref, wdb_ref, hg_ref, hu_ref,
                   wgb_ref, wub_ref, *, tiles_per_seq):
    m = pl.program_id(1)
    tm = xn_ref.shape[0]
    wdb_ref[...] = wd_ref[...].astype(BF16)
    for h_ref in (hg_ref, hu_ref):
        @pl.when(m % tiles_per_seq == 0)
        def _():
            h_ref[0:CONV_HALO, :] = jnp.zeros((CONV_HALO, h_ref.shape[1]), F32)

        @pl.when(m % tiles_per_seq != 0)
        def _():
            h_ref[0:CONV_HALO, :] = h_ref[tm:tm + CONV_HALO, :]

    def conv(h_ref, c_ref, b_ref, r0):
        h = h_ref[r0:r0 + CONV_HALO + FFN_ROWS, :]
        out = b_ref[...] + c_ref[CONV_WIDTH - 1:CONV_WIDTH, :] * h[CONV_HALO:]
        for back in range(1, CONV_WIDTH):
            shifted = pltpu.roll(h, back, axis=0)[CONV_HALO:]
            out = out + c_ref[CONV_WIDTH - 1 - back:CONV_WIDTH - back, :] * shifted
        return out

    @pl.when(m == 0)
    def _():
        wgb_ref[...] = wg_ref[...].astype(BF16)
        wub_ref[...] = wu_ref[...].astype(BF16)

    for r0 in range(0, tm, FFN_ROWS):
        xn = xn_ref[r0:r0 + FFN_ROWS, :]
        for h_ref, w_ref in ((hg_ref, wgb_ref), (hu_ref, wub_ref)):
            h_ref[CONV_HALO + r0:CONV_HALO + r0 + FFN_ROWS, :] = jnp.dot(
                xn, w_ref[...], preferred_element_type=F32)
        gate = conv(hg_ref, cg_ref, bg_ref, r0)
        up = conv(hu_ref, cu_ref, bu_ref, r0)
        o_ref[r0:r0 + FFN_ROWS, :] = (gate * jax.nn.sigmoid(gate) * up).astype(BF16)


def _ffn_up(xn2, w_up, conv_w, conv_b, w_down, seq, tm=2048, tn=512):
    t = xn2.shape[0]
    nn = D_FF // tn
    nm = t // tm
    job = _row_cast(w_down, nn * nm)
    kern = functools.partial(_ffn_up_kernel, tiles_per_seq=seq // tm)
    return pl.pallas_call(
        kern,
        grid=(nn, nm),
        in_specs=[
            pl.BlockSpec((tm, D_MODEL), lambda n, m: (m, 0)),
            pl.BlockSpec((D_MODEL, tn), lambda n, m: (0, n)),
            pl.BlockSpec((D_MODEL, tn), lambda n, m: (0, nn + n)),
            pl.BlockSpec((CONV_WIDTH, tn), lambda n, m: (0, n)),
            pl.BlockSpec((CONV_WIDTH, tn), lambda n, m: (0, nn + n)),
            pl.BlockSpec((1, tn), lambda n, m: (0, n)),
            pl.BlockSpec((1, tn), lambda n, m: (0, nn + n)),
            pl.BlockSpec(job.block, lambda n, m: job.src_index(n * nm + m)),
        ],
        out_specs=[
            pl.BlockSpec((tm, tn), lambda n, m: (m, n)),
            pl.BlockSpec(job.block, lambda n, m: job.out_index(n * nm + m)),
        ],
        out_shape=[jax.ShapeDtypeStruct((t, D_FF), BF16), jax.ShapeDtypeStruct(job.out_shape, BF16)],
        scratch_shapes=[
            pltpu.VMEM((CONV_HALO + tm, tn), F32),
            pltpu.VMEM((CONV_HALO + tm, tn), F32),
            pltpu.VMEM((D_MODEL, tn), BF16),
            pltpu.VMEM((D_MODEL, tn), BF16),
        ],
        compiler_params=_params("arbitrary", "arbitrary"),
        name="ffn_up",
    )(xn2, w_up, w_up, conv_w, conv_w, conv_b, conv_b, w_down)


def _ffn_down_kernel(h_ref, act_ref, w_ref, g_ref, o_ref):
    h = h_ref[...] + jnp.dot(act_ref[...], w_ref[...], preferred_element_type=F32)
    o_ref[...] = _rms(h, g_ref[...])


def _ffn_down(h1, act, w_down, g, tm=256):
    t = h1.shape[0]
    return pl.pallas_call(
        _ffn_down_kernel,
        grid=(t // tm,),
        in_specs=[
            pl.BlockSpec((tm, D_MODEL), lambda m: (m, 0)),
            pl.BlockSpec((tm, D_FF), lambda m: (m, 0)),
            _resident((D_FF, D_MODEL), lambda m: (0, 0)),
            _resident((1, D_MODEL), lambda m: (0, 0)),
        ],
        out_specs=pl.BlockSpec((tm, D_MODEL), lambda m: (m, 0)),
        out_shape=jax.ShapeDtypeStruct((t, D_MODEL), F32),
        compiler_params=_params("arbitrary"),
        name="ffn_down",
    )(h1, act, w_down, g)


def kernel(x, norm_mix_g, w_in, w_sb_out, w_swa_out, w_o, sinks, norm_ffn_g, w_up, conv_w, conv_b, w_down, norm_final_g):
    batch, seq, _ = x.shape
    assert w_in.shape[0] == 1, "single-layer block: the final norm is fused into the FFN down projection"
    row = lambda v: v.reshape(1, -1)
    x2 = x.reshape(batch * seq, D_MODEL)
    w_in2 = w_in.reshape(D_MODEL, -1)
    xn, qkv, vt = _inproj(x2, row(norm_mix_g[0]), w_in2[:, :QKV_W].astype(BF16), batch, seq)

    sb_steps = batch * (SB_W // (SB_GROUP * HEAD_DIM)) * (seq // SB_BLK)
    gate_rows, gate_cols = 512, 256
    col_blocks = 2 * D_MODEL // gate_cols
    assert QKV_W % gate_cols == 0 and (D_MODEL // gate_rows) * col_blocks == sb_steps
    gates_job = _CastJob(
        w_in2, (gate_rows, gate_cols),
        lambda s: (s // col_blocks, QKV_W // gate_cols + s % col_blocks),
        (D_MODEL, 2 * D_MODEL), lambda s: (s // col_blocks, s % col_blocks))
    jobs = [gates_job, _row_cast(w_sb_out.reshape(SB_W, D_MODEL), sb_steps),
            _row_cast(w_swa_out.reshape(SWA_Q_W, D_MODEL), sb_steps), _row_cast(w_o.reshape(D_MODEL, D_MODEL), sb_steps)]
    att_a, (w_gates, w_a, w_b, w_ob) = _sb_attention(qkv, vt, batch, seq, jobs)
    att_b = _swa_attention(qkv, sinks[0], batch, seq)
    mixed = _mix(xn, w_gates, att_a, w_a, att_b, w_b)
    h1, xn2 = _oproj(x2, mixed, w_ob, row(norm_ffn_g[0]))
    act, w_db = _ffn_up(xn2, w_up.reshape(D_MODEL, 2 * D_FF), conv_w[0], row(conv_b[0]), w_down.reshape(D_FF, D_MODEL), seq)
    out = _ffn_down(h1, act, w_db, row(norm_final_g))
    return out.reshape(batch, seq, D_MODEL)
```

```python
import functools
import math
from typing import Callable, NamedTuple

import jax
import jax.numpy as jnp
import numpy as np
from jax import lax
from jax.experimental import pallas as pl
from jax.experimental.pallas import tpu as pltpu

D_MODEL = 2048
HEAD_DIM = 64
SB_HEADS = 16
SWA_Q_HEADS = 16
SWA_KV_HEADS = 2
SWA_GROUP = SWA_Q_HEADS // SWA_KV_HEADS
WINDOW = 128
D_FF = 5632
CONV_WIDTH = 3
RMS_EPS = 1e-5
NEG_INF = -1e30

SB_W = SB_HEADS * HEAD_DIM
SWA_Q_W = SWA_Q_HEADS * HEAD_DIM
SWA_KV_W = SWA_KV_HEADS * HEAD_DIM
QKV_W = 3 * SB_W + SWA_Q_W + 2 * SWA_KV_W
LANES = 128

BF16 = jnp.bfloat16
F32 = jnp.float32

VMEM_LIMIT = 56 * 1024 * 1024


def _params(*sem):
    return pltpu.CompilerParams(dimension_semantics=sem, vmem_limit_bytes=VMEM_LIMIT)


def _resident(shape, index_map):
    return pl.BlockSpec(shape, index_map, pipeline_mode=pl.Buffered(1))


def _rms(xf, g):
    return xf * lax.rsqrt(jnp.mean(xf * xf, axis=-1, keepdims=True) + RMS_EPS) * g


SB_BLK = 256


def _inproj_kernel(x_ref, g_ref, w_ref, xn_ref, qkv_ref, vt_ref):
    xn = _rms(x_ref[...], g_ref[...]).astype(BF16)
    xn_ref[...] = xn
    qkv = jnp.dot(xn, w_ref[...], preferred_element_type=F32)
    qkv_ref[...] = qkv.astype(BF16)
    vt = qkv[:, 2 * SB_W:3 * SB_W].T.astype(BF16)
    for kb in range(vt_ref.shape[1]):
        vt_ref[0, kb] = vt[:, kb * SB_BLK:(kb + 1) * SB_BLK]


def _inproj(x2, g, w_qkv, batch, seq, tm=512):
    t = x2.shape[0]
    tiles_per_seq = seq // tm
    return pl.pallas_call(
        _inproj_kernel,
        grid=(t // tm,),
        in_specs=[
            pl.BlockSpec((tm, D_MODEL), lambda m: (m, 0)),
            _resident((1, D_MODEL), lambda m: (0, 0)),
            _resident((D_MODEL, QKV_W), lambda m: (0, 0)),
        ],
        out_specs=[
            pl.BlockSpec((tm, D_MODEL), lambda m: (m, 0)),
            pl.BlockSpec((tm, QKV_W), lambda m: (m, 0)),
            pl.BlockSpec((1, tm // SB_BLK, SB_W, SB_BLK), lambda m: (m // tiles_per_seq, m % tiles_per_seq, 0, 0)),
        ],
        out_shape=[
            jax.ShapeDtypeStruct((t, D_MODEL), BF16),
            jax.ShapeDtypeStruct((t, QKV_W), BF16),
            jax.ShapeDtypeStruct((batch, seq // SB_BLK, SB_W, SB_BLK), BF16),
        ],
        compiler_params=_params("arbitrary"),
        name="inproj",
    )(x2, g, w_qkv)


SB_GROUP = 8
SB_CHUNK = 256


def _sb_kernel(q_ref, k_ref, vt_ref, *rest, blk, n_casts):
    cast_srcs, o_ref, cast_dsts = rest[:n_casts], rest[n_casts], rest[n_casts + 1:2 * n_casts + 1]
    za_ref, zb_ref, wa_ref, wb_ref, carry_ref, acc_ref = rest[2 * n_casts + 1:]
    for src, dst in zip(cast_srcs, cast_dsts):
        dst[...] = src[...].astype(BF16)
    i = pl.program_id(2)
    width = SB_GROUP * HEAD_DIM
    n = SB_GROUP * blk
    q = q_ref[...] * BF16(1.0 / math.sqrt(HEAD_DIM))
    qt = q.astype(F32).T.astype(BF16)
    chunk_heads = SB_CHUNK // HEAD_DIM
    head_of_row = lax.broadcasted_iota(jnp.int32, (SB_CHUNK, blk), 0) // HEAD_DIM
    qstacks = []
    for cc in range(width // SB_CHUNK):
        qt_c = qt[cc * SB_CHUNK:(cc + 1) * SB_CHUNK]
        qstacks.append(jnp.concatenate(
            [jnp.where(head_of_row == c, qt_c, jnp.zeros_like(qt_c)) for c in range(chunk_heads)], axis=1))
    r = lax.broadcasted_iota(jnp.int32, (blk, blk), 0)
    c = lax.broadcasted_iota(jnp.int32, (blk, blk), 1)
    later = (c > r).astype(BF16)
    key_idx = lax.broadcasted_iota(jnp.int32, (blk, n), 0)
    query_idx = lax.broadcasted_iota(jnp.int32, (blk, n), 1) & (blk - 1)
    causal = key_idx < query_idx

    def scores(j, z_ref):
        kb = k_ref[pl.ds(j * blk, blk), :]
        for cc, qs in enumerate(qstacks):
            z_ref[:, cc * chunk_heads * blk:(cc + 1) * chunk_heads * blk] = jnp.dot(
                kb[:, cc * SB_CHUNK:(cc + 1) * SB_CHUNK], qs, preferred_element_type=F32)

    def weights(z_ref, a_ref, masked):
        z = z_ref[...]
        carry = carry_ref[...]
        sp = jnp.where(z > 17.0, z, jnp.log(1.0 + jnp.exp(z)))
        log_beta = z - sp
        if masked:
            sp = jnp.where(causal, sp, 0.0)
        later_sum = jnp.dot(later, sp.astype(BF16), preferred_element_type=F32)
        a = jnp.exp((log_beta - later_sum - carry).astype(BF16))
        if masked:
            a = jnp.where(causal, a, jnp.zeros_like(a))
        a_ref[...] = a
        carry_ref[...] = carry + later_sum[0:1, :] + sp[0:1, :]

    def values(j, a_ref):
        for h in range(SB_GROUP):
            vt = vt_ref[0, j, h * HEAD_DIM:(h + 1) * HEAD_DIM, :]
            acc_ref[h * HEAD_DIM:(h + 1) * HEAD_DIM, :] += jnp.dot(
                vt, a_ref[:, h * blk:(h + 1) * blk], preferred_element_type=F32)

    carry_ref[...] = jnp.zeros_like(carry_ref)
    acc_ref[...] = jnp.zeros_like(acc_ref)
    first = i - 1 - i % 2

    @pl.when(i % 2 == 0)
    def _():
        scores(i, zb_ref)
        scores(jnp.maximum(first, 0), za_ref)
        weights(zb_ref, wb_ref, True)

    @pl.when(i % 2 == 1)
    def _():
        scores(i, za_ref)
        scores(i - 1, zb_ref)
        weights(za_ref, wa_ref, True)
        scores(jnp.maximum(first, 0), za_ref)
        values(i, wa_ref)
        weights(zb_ref, wb_ref, False)

    def pair(p, _):
        j = first - 2 * p
        scores(j - 1, zb_ref)
        values(j + 1, wb_ref)
        weights(za_ref, wa_ref, False)
        scores(jnp.maximum(j - 2, 0), za_ref)
        values(j, wa_ref)
        weights(zb_ref, wb_ref, False)
        return 0

    lax.fori_loop(0, i // 2, pair, 0)
    values(0, wb_ref)
    o_ref[...] = acc_ref[...].T.astype(BF16)


class _CastJob(NamedTuple):
    src: jax.Array
    block: tuple
    src_index: Callable
    out_shape: tuple
    out_index: Callable


def _row_cast(src, steps):
    rows, cols = src.shape
    assert rows % (steps * 16) == 0, (rows, steps)
    return _CastJob(src, (rows // steps, cols), lambda s: (s, 0), (rows, cols), lambda s: (s, 0))


def _sb_attention(qkv, vt, batch, seq, cast_jobs, blk=SB_BLK):
    nq = seq // blk
    width = SB_GROUP * HEAD_DIM
    groups = SB_W // width
    step = lambda b, g, i: (b * groups + g) * nq + i
    at_step = lambda index: (lambda b, g, i: index(step(b, g, i)))
    outs = pl.pallas_call(
        functools.partial(_sb_kernel, blk=blk, n_casts=len(cast_jobs)),
        grid=(batch, groups, nq),
        in_specs=[
            pl.BlockSpec((blk, width), lambda b, g, i: (b * nq + i, g)),
            pl.BlockSpec((seq, width), lambda b, g, i: (b, groups + g)),
            pl.BlockSpec((1, nq, width, blk), lambda b, g, i: (b, 0, g, 0)),
        ] + [pl.BlockSpec(job.block, at_step(job.src_index)) for job in cast_jobs],
        out_specs=[pl.BlockSpec((blk, width), lambda b, g, i: (b * nq + i, g))]
        + [pl.BlockSpec(job.block, at_step(job.out_index)) for job in cast_jobs],
        out_shape=[jax.ShapeDtypeStruct((batch * seq, SB_W), BF16)]
        + [jax.ShapeDtypeStruct(job.out_shape, BF16) for job in cast_jobs],
        scratch_shapes=[
            pltpu.VMEM((blk, SB_GROUP * blk), F32),
            pltpu.VMEM((blk, SB_GROUP * blk), F32),
            pltpu.VMEM((blk, SB_GROUP * blk), BF16),
            pltpu.VMEM((blk, SB_GROUP * blk), BF16),
            pltpu.VMEM((1, SB_GROUP * blk), F32),
            pltpu.VMEM((width, blk), F32),
        ],
        compiler_params=_params("arbitrary", "arbitrary", "arbitrary"),
        name="sb_attn",
    )(qkv, qkv, vt, *[job.src for job in cast_jobs])
    return outs[0], outs[1:]


def _swa_kernel(sink_ref, q_ref, kp_ref, kc_ref, vp_ref, vc_ref, o_ref, bias_ref, *, slopes):
    blk = q_ref.shape[0]
    nk = 2 * blk
    first_step = (pl.program_id(0) == 0) & (pl.program_id(1) == 0)

    @pl.when(first_step)
    def _():
        key = lax.broadcasted_iota(jnp.int32, (nk, blk), 0)
        qry = lax.broadcasted_iota(jnp.int32, (nk, blk), 1)
        dist = qry - key + blk
        in_window = (dist >= 0) & (dist < WINDOW)
        distf = dist.astype(F32)
        for h in range(SWA_Q_HEADS):
            general = jnp.where(in_window, -(slopes[h] * distf), NEG_INF)
            bias_ref[1, :, h * blk:(h + 1) * blk] = general
            bias_ref[0, :, h * blk:(h + 1) * blk] = jnp.where(key >= blk, general, NEG_INF)

    table = jnp.minimum(pl.program_id(1), 1)
    q = q_ref[...] * BF16(1.0 / math.sqrt(HEAD_DIM))
    qt = q.astype(F32).T.astype(BF16)
    zeros = jnp.zeros((HEAD_DIM, blk), BF16)
    cols = []
    for h in range(SWA_Q_HEADS):
        qh = qt[h * HEAD_DIM:(h + 1) * HEAD_DIM]
        cols.append(jnp.concatenate([qh, zeros] if h // SWA_GROUP == 0 else [zeros, qh], axis=0))
    qstack = jnp.concatenate(cols, axis=1)
    kb = jnp.concatenate([kp_ref[...], kc_ref[...]], axis=0)
    z = jnp.dot(kb, qstack, preferred_element_type=F32) + bias_ref[table]
    sink_row = jnp.concatenate([jnp.full((1, blk), sink_ref[h], F32) for h in range(SWA_Q_HEADS)], axis=1)
    zmax = jnp.maximum(jnp.max(z, axis=0, keepdims=True), sink_row)
    p = jnp.exp((z - zmax).astype(BF16))
    sink_term = jnp.exp(sink_row - zmax)
    vt = jnp.concatenate([vp_ref[...], vc_ref[...]], axis=0).astype(F32).T
    ones = jnp.ones((HEAD_DIM, nk), F32)
    gw = SWA_GROUP * blk
    outs = []
    for g in range(SWA_KV_HEADS):
        vext = jnp.concatenate([vt[g * HEAD_DIM:(g + 1) * HEAD_DIM], ones], axis=0).astype(BF16)
        ot = jnp.dot(vext, p[:, g * gw:(g + 1) * gw], preferred_element_type=F32)
        denom = ot[HEAD_DIM:HEAD_DIM + 1, :] + sink_term[:, g * gw:(g + 1) * gw]
        res = ot[:HEAD_DIM] / denom
        outs += [res[:, k * blk:(k + 1) * blk] for k in range(SWA_GROUP)]
    o_ref[...] = jnp.concatenate(outs, axis=0).T.astype(BF16)


def _swa_attention(qkv, sinks, batch, seq, blk=128):
    nb = seq // blk
    qcol = 3 * SB_W // SWA_Q_W
    kcol = (3 * SB_W + SWA_Q_W) // SWA_KV_W
    vcol = kcol + 1
    slopes = [float(v) for v in np.power(2.0, -8.0 * np.arange(1, SWA_Q_HEADS + 1) / SWA_Q_HEADS).astype(np.float32)]
    cur = lambda col: (lambda b, n: (b * nb + n, col))
    prev = lambda col: (lambda b, n: (b * nb + jnp.maximum(n - 1, 0), col))
    return pl.pallas_call(
        functools.partial(_swa_kernel, slopes=slopes),
        grid=(batch, nb),
        in_specs=[
            pl.BlockSpec(memory_space=pltpu.SMEM),
            pl.BlockSpec((blk, SWA_Q_W), cur(qcol)),
            pl.BlockSpec((blk, SWA_KV_W), prev(kcol)),
            pl.BlockSpec((blk, SWA_KV_W), cur(kcol)),
            pl.BlockSpec((blk, SWA_KV_W), prev(vcol)),
            pl.BlockSpec((blk, SWA_KV_W), cur(vcol)),
        ],
        out_specs=pl.BlockSpec((blk, SWA_Q_W), lambda b, n: (b * nb + n, 0)),
        out_shape=jax.ShapeDtypeStruct((batch * seq, SWA_Q_W), BF16),
        scratch_shapes=[pltpu.VMEM((2, 2 * blk, SWA_Q_HEADS * blk), F32)],
        compiler_params=_params("arbitrary", "arbitrary"),
        name="swa_attn",
    )(sinks, qkv, qkv, qkv, qkv, qkv)


def _mix_kernel(xn_ref, wga_ref, wgb_ref, a_ref, wa_ref, b_ref, wb_ref, o_ref):
    xn = xn_ref[...]
    ga = jnp.dot(xn, wga_ref[...], preferred_element_type=F32)
    gb = jnp.dot(xn, wgb_ref[...], preferred_element_type=F32)
    ya = jnp.dot(a_ref[...], wa_ref[...], preferred_element_type=F32)
    yb = jnp.dot(b_ref[...], wb_ref[...], preferred_element_type=F32)
    o_ref[...] = (jax.nn.sigmoid(ga) * ya + jax.nn.sigmoid(gb) * yb).astype(BF16)


def _mix(xn, w_gates, att_a, w_a, att_b, w_b, tm=1024, tn=1024):
    t = xn.shape[0]
    nn = D_MODEL // tn
    return pl.pallas_call(
        _mix_kernel,
        grid=(t // tm, nn),
        in_specs=[
            pl.BlockSpec((tm, D_MODEL), lambda m, n: (m, 0)),
            pl.BlockSpec((D_MODEL, tn), lambda m, n: (0, n)),
            pl.BlockSpec((D_MODEL, tn), lambda m, n: (0, nn + n)),
            pl.BlockSpec((tm, SB_W), lambda m, n: (m, 0)),
            pl.BlockSpec((SB_W, tn), lambda m, n: (0, n)),
            pl.BlockSpec((tm, SWA_Q_W), lambda m, n: (m, 0)),
            pl.BlockSpec((SWA_Q_W, tn), lambda m, n: (0, n)),
        ],
        out_specs=pl.BlockSpec((tm, tn), lambda m, n: (m, n)),
        out_shape=jax.ShapeDtypeStruct((t, D_MODEL), BF16),
        compiler_params=_params("arbitrary", "arbitrary"),
        name="mix",
    )(xn, w_gates, w_gates, att_a, w_a, att_b, w_b)


def _oproj_kernel(x_ref, mixed_ref, w_ref, g_ref, h_ref, xn_ref):
    h = x_ref[...] + jnp.dot(mixed_ref[...], w_ref[...], preferred_element_type=F32)
    h_ref[...] = h
    xn_ref[...] = _rms(h, g_ref[...]).astype(BF16)


def _oproj(x2, mixed, w_o, g, tm=512):
    t = x2.shape[0]
    return pl.pallas_call(
        _oproj_kernel,
        grid=(t // tm,),
        in_specs=[
            pl.BlockSpec((tm, D_MODEL), lambda m: (m, 0)),
            pl.BlockSpec((tm, D_MODEL), lambda m: (m, 0)),
            _resident((D_MODEL, D_MODEL), lambda m: (0, 0)),
            _resident((1, D_MODEL), lambda m: (0, 0)),
        ],
        out_specs=[
            pl.BlockSpec((tm, D_MODEL), lambda m: (m, 0)),
            pl.BlockSpec((tm, D_MODEL), lambda m: (m, 0)),
        ],
        out_shape=[
            jax.ShapeDtypeStruct((t, D_MODEL), F32),
            jax.ShapeDtypeStruct((t, D_MODEL), BF16),
        ],
        compiler_params=_params("arbitrary"),
        name="oproj",
    )(x2, mixed, w_o, g)


CONV_HALO = 8
FFN_ROWS = 256


def _ffn_up_kernel(xn_ref, wg_ref, wu_ref, cg_ref, cu_ref, bg_ref, bu_ref, wd_ref, o_ref, wdb_ref, hg_ref, hu_ref,
                   wgb_ref, wub_ref, *, tiles_per_seq):
    m = pl.program_id(1)
    tm = xn_ref.shape[0]
    wdb_ref[...] = wd_ref[...].astype(BF16)
    for h_ref in (hg_ref, hu_ref):
        @pl.when(m % tiles_per_seq == 0)
        def _():
            h_ref[0:CONV_HALO, :] = jnp.zeros((CONV_HALO, h_ref.shape[1]), F32)

        @pl.when(m % tiles_per_seq != 0)
        def _():
            h_ref[0:CONV_HALO, :] = h_ref[tm:tm + CONV_HALO, :]

    def conv(h_ref, c_ref, b_ref, r0):
        h = h_ref[r0:r0 + CONV_HALO + FFN_ROWS, :]
        out = b_ref[...] + c_ref[CONV_WIDTH - 1:CONV_WIDTH, :] * h[CONV_HALO:]
        for back in range(1, CONV_WIDTH):
            shifted = pltpu.roll(h, back, axis=0)[CONV_HALO:]
            out = out + c_ref[CONV_WIDTH - 1 - back:CONV_WIDTH - back, :] * shifted
        return out

    @pl.when(m == 0)
    def _():
        wgb_ref[...] = wg_ref[...].astype(BF16)
        wub_ref[...] = wu_ref[...].astype(BF16)

    for r0 in range(0, tm, FFN_ROWS):
        xn = xn_ref[r0:r0 + FFN_ROWS, :]
        for h_ref, w_ref in ((hg_ref, wgb_ref), (hu_ref, wub_ref)):
            h_ref[CONV_HALO + r0:CONV_HALO + r0 + FFN_ROWS, :] = jnp.dot(
                xn, w_ref[...], preferred_element_type=F32)
        gate = conv(hg_ref, cg_ref, bg_ref, r0)
        up = conv(hu_ref, cu_ref, bu_ref, r0)
        o_ref[r0:r0 + FFN_ROWS, :] = (gate * jax.nn.sigmoid(gate) * up).astype(BF16)


def _ffn_up(xn2, w_up, conv_w, conv_b, w_down, seq, tm=2048, tn=512):
    t = xn2.shape[0]
    nn = D_FF // tn
    nm = t // tm
    job = _row_cast(w_down, nn * nm)
    kern = functools.partial(_ffn_up_kernel, tiles_per_seq=seq // tm)
    return pl.pallas_call(
        kern,
        grid=(nn, nm),
        in_specs=[
            pl.BlockSpec((tm, D_MODEL), lambda n, m: (m, 0)),
            pl.BlockSpec((D_MODEL, tn), lambda n, m: (0, n)),
            pl.BlockSpec((D_MODEL, tn), lambda n, m: (0, nn + n)),
            pl.BlockSpec((CONV_WIDTH, tn), lambda n, m: (0, n)),
            pl.BlockSpec((CONV_WIDTH, tn), lambda n, m: (0, nn + n)),
            pl.BlockSpec((1, tn), lambda n, m: (0, n)),
            pl.BlockSpec((1, tn), lambda n, m: (0, nn + n)),
            pl.BlockSpec(job.block, lambda n, m: job.src_index(n * nm + m)),
        ],
        out_specs=[
            pl.BlockSpec((tm, tn), lambda n, m: (m, n)),
            pl.BlockSpec(job.block, lambda n, m: job.out_index(n * nm + m)),
        ],
        out_shape=[jax.ShapeDtypeStruct((t, D_FF), BF16), jax.ShapeDtypeStruct(job.out_shape, BF16)],
        scratch_shapes=[
            pltpu.VMEM((CONV_HALO + tm, tn), F32),
            pltpu.VMEM((CONV_HALO + tm, tn), F32),
            pltpu.VMEM((D_MODEL, tn), BF16),
            pltpu.VMEM((D_MODEL, tn), BF16),
        ],
        compiler_params=_params("arbitrary", "arbitrary"),
        name="ffn_up",
    )(xn2, w_up, w_up, conv_w, conv_w, conv_b, conv_b, w_down)


def _ffn_down_kernel(h_ref, act_ref, w_ref, g_ref, o_ref):
    h = h_ref[...] + jnp.dot(act_ref[...], w_ref[...], preferred_element_type=F32)
    o_ref[...] = _rms(h, g_ref[...])


def _ffn_down(h1, act, w_down, g, tm=256):
    t = h1.shape[0]
    return pl.pallas_call(
        _ffn_down_kernel,
        grid=(t // tm,),
        in_specs=[
            pl.BlockSpec((tm, D_MODEL), lambda m: (m, 0)),
            pl.BlockSpec((tm, D_FF), lambda m: (m, 0)),
            _resident((D_FF, D_MODEL), lambda m: (0, 0)),
            _resident((1, D_MODEL), lambda m: (0, 0)),
        ],
        out_specs=pl.BlockSpec((tm, D_MODEL), lambda m: (m, 0)),
        out_shape=jax.ShapeDtypeStruct((t, D_MODEL), F32),
        compiler_params=_params("arbitrary"),
        name="ffn_down",
    )(h1, act, w_down, g)


def kernel(x, norm_mix_g, w_in, w_sb_out, w_swa_out, w_o, sinks, norm_ffn_g, w_up, conv_w, conv_b, w_down, norm_final_g):
    batch, seq, _ = x.shape
    assert w_in.shape[0] == 1, "single-layer block: the final norm is fused into the FFN down projection"
    row = lambda v: v.reshape(1, -1)
    x2 = x.reshape(batch * seq, D_MODEL)
    w_in2 = w_in.reshape(D_MODEL, -1)
    xn, qkv, vt = _inproj(x2, row(norm_mix_g[0]), w_in2[:, :QKV_W].astype(BF16), batch, seq)

    sb_steps = batch * (SB_W // (SB_GROUP * HEAD_DIM)) * (seq // SB_BLK)
    gate_rows, gate_cols = 512, 256
    col_blocks = 2 * D_MODEL // gate_cols
    assert QKV_W % gate_cols == 0 and (D_MODEL // gate_rows) * col_blocks == sb_steps
    gates_job = _CastJob(
        w_in2, (gate_rows, gate_cols),
        lambda s: (s // col_blocks, QKV_W // gate_cols + s % col_blocks),
        (D_MODEL, 2 * D_MODEL), lambda s: (s // col_blocks, s % col_blocks))
    jobs = [gates_job, _row_cast(w_sb_out.reshape(SB_W, D_MODEL), sb_steps),
            _row_cast(w_swa_out.reshape(SWA_Q_W, D_MODEL), sb_steps), _row_cast(w_o.reshape(D_MODEL, D_MODEL), sb_steps)]
    att_a, (w_gates, w_a, w_b, w_ob) = _sb_attention(qkv, vt, batch, seq, jobs)
    att_b = _swa_attention(qkv, sinks[0], batch, seq)
    mixed = _mix(xn, w_gates, att_a, w_a, att_b, w_b)
    h1, xn2 = _oproj(x2, mixed, w_ob, row(norm_ffn_g[0]))
    act, w_db = _ffn_up(xn2, w_up.reshape(D_MODEL, 2 * D_FF), conv_w[0], row(conv_b[0]), w_down.reshape(D_FF, D_MODEL), seq)
    out = _ffn_down(h1, act, w_db, row(norm_final_g))
    return out.reshape(batch, seq, D_MODEL)
```

```python
import functools
import math
from typing import Callable, NamedTuple

import jax
import jax.numpy as jnp
import numpy as np
from jax import lax
from jax.experimental import pallas as pl
from jax.experimental.pallas import tpu as pltpu

D_MODEL = 2048
HEAD_DIM = 64
SB_HEADS = 16
SWA_Q_HEADS = 16
SWA_KV_HEADS = 2
SWA_GROUP = SWA_Q_HEADS // SWA_KV_HEADS
WINDOW = 128
D_FF = 5632
CONV_WIDTH = 3
RMS_EPS = 1e-5
NEG_INF = -1e30

SB_W = SB_HEADS * HEAD_DIM
SWA_Q_W = SWA_Q_HEADS * HEAD_DIM
SWA_KV_W = SWA_KV_HEADS * HEAD_DIM
QKV_W = 3 * SB_W + SWA_Q_W + 2 * SWA_KV_W
LANES = 128

BF16 = jnp.bfloat16
F32 = jnp.float32

VMEM_LIMIT = 60 * 1024 * 1024


def _params(*sem):
    return pltpu.CompilerParams(dimension_semantics=sem, vmem_limit_bytes=VMEM_LIMIT)


def _resident(shape, index_map):
    return pl.BlockSpec(shape, index_map, pipeline_mode=pl.Buffered(1))


def _rms(xf, g):
    return xf * lax.rsqrt(jnp.mean(xf * xf, axis=-1, keepdims=True) + RMS_EPS) * g


SB_BLK = 256


def _inproj_kernel(x_ref, g_ref, w_ref, xn_ref, qkv_ref, vt_ref):
    xn = _rms(x_ref[...], g_ref[...]).astype(BF16)
    xn_ref[...] = xn
    qkv = jnp.dot(xn, w_ref[...], preferred_element_type=F32)
    qkv_ref[...] = qkv.astype(BF16)
    vt = qkv[:, 2 * SB_W:3 * SB_W].T.astype(BF16)
    for kb in range(vt_ref.shape[1]):
        vt_ref[0, kb] = vt[:, kb * SB_BLK:(kb + 1) * SB_BLK]


def _inproj(x2, g, w_qkv, batch, seq, tm=512):
    t = x2.shape[0]
    tiles_per_seq = seq // tm
    return pl.pallas_call(
        _inproj_kernel,
        grid=(t // tm,),
        in_specs=[
            pl.BlockSpec((tm, D_MODEL), lambda m: (m, 0)),
            _resident((1, D_MODEL), lambda m: (0, 0)),
            _resident((D_MODEL, QKV_W), lambda m: (0, 0)),
        ],
        out_specs=[
            pl.BlockSpec((tm, D_MODEL), lambda m: (m, 0)),
            pl.BlockSpec((tm, QKV_W), lambda m: (m, 0)),
            pl.BlockSpec((1, tm // SB_BLK, SB_W, SB_BLK), lambda m: (m // tiles_per_seq, m % tiles_per_seq, 0, 0)),
        ],
        out_shape=[
            jax.ShapeDtypeStruct((t, D_MODEL), BF16),
            jax.ShapeDtypeStruct((t, QKV_W), BF16),
            jax.ShapeDtypeStruct((batch, seq // SB_BLK, SB_W, SB_BLK), BF16),
        ],
        compiler_params=_params("arbitrary"),
        name="inproj",
    )(x2, g, w_qkv)


SB_GROUP = 8
SB_CHUNK = 256


def _sb_kernel(q_ref, k_ref, vt_ref, *rest, blk, n_casts):
    cast_srcs, o_ref, cast_dsts = rest[:n_casts], rest[n_casts], rest[n_casts + 1:2 * n_casts + 1]
    za_ref, zb_ref, wa_ref, wb_ref, carry_ref, acc_ref = rest[2 * n_casts + 1:]
    for src, dst in zip(cast_srcs, cast_dsts):
        dst[...] = src[...].astype(BF16)
    i = pl.program_id(2)
    width = SB_GROUP * HEAD_DIM
    n = SB_GROUP * blk
    q = q_ref[...] * BF16(1.0 / math.sqrt(HEAD_DIM))
    qt = q.astype(F32).T.astype(BF16)
    chunk_heads = SB_CHUNK // HEAD_DIM
    head_of_row = lax.broadcasted_iota(jnp.int32, (SB_CHUNK, blk), 0) // HEAD_DIM
    qstacks = []
    for cc in range(width // SB_CHUNK):
        qt_c = qt[cc * SB_CHUNK:(cc + 1) * SB_CHUNK]
        qstacks.append(jnp.concatenate(
            [jnp.where(head_of_row == c, qt_c, jnp.zeros_like(qt_c)) for c in range(chunk_heads)], axis=1))
    r = lax.broadcasted_iota(jnp.int32, (blk, blk), 0)
    c = lax.broadcasted_iota(jnp.int32, (blk, blk), 1)
    later = (c > r).astype(BF16)
    key_idx = lax.broadcasted_iota(jnp.int32, (blk, n), 0)
    query_idx = lax.broadcasted_iota(jnp.int32, (blk, n), 1) & (blk - 1)
    causal = key_idx < query_idx

    def scores(j, z_ref):
        kb = k_ref[pl.ds(j * blk, blk), :]
        for cc, qs in enumerate(qstacks):
            z_ref[:, cc * chunk_heads * blk:(cc + 1) * chunk_heads * blk] = jnp.dot(
                kb[:, cc * SB_CHUNK:(cc + 1) * SB_CHUNK], qs, preferred_element_type=F32)

    def weights(z_ref, a_ref, masked):
        z = z_ref[...]
        carry = carry_ref[...]
        sp = jnp.where(z > 17.0, z, jnp.log(1.0 + jnp.exp(z)))
        log_beta = z - sp
        if masked:
            sp = jnp.where(causal, sp, 0.0)
        later_sum = jnp.dot(later, sp.astype(BF16), preferred_element_type=F32)
        a = jnp.exp((log_beta - later_sum - carry).astype(BF16))
        if masked:
            a = jnp.where(causal, a, jnp.zeros_like(a))
        a_ref[...] = a
        carry_ref[...] = carry + later_sum[0:1, :] + sp[0:1, :]

    def values(j, a_ref):
        for h in range(SB_GROUP):
            vt = vt_ref[0, j, h * HEAD_DIM:(h + 1) * HEAD_DIM, :]
            acc_ref[h * HEAD_DIM:(h + 1) * HEAD_DIM, :] += jnp.dot(
                vt, a_ref[:, h * blk:(h + 1) * blk], preferred_element_type=F32)

    carry_ref[...] = jnp.zeros_like(carry_ref)
    acc_ref[...] = jnp.zeros_like(acc_ref)
    first = i - 1 - i % 2

    @pl.when(i % 2 == 0)
    def _():
        scores(i, zb_ref)
        scores(jnp.maximum(first, 0), za_ref)
        weights(zb_ref, wb_ref, True)

    @pl.when(i % 2 == 1)
    def _():
        scores(i, za_ref)
        scores(i - 1, zb_ref)
        weights(za_ref, wa_ref, True)
        scores(jnp.maximum(first, 0), za_ref)
        values(i, wa_ref)
        weights(zb_ref, wb_ref, False)

    def pair(p, _):
        j = first - 2 * p
        scores(j - 1, zb_ref)
        values(j + 1, wb_ref)
        weights(za_ref, wa_ref, False)
        scores(jnp.maximum(j - 2, 0), za_ref)
        values(j, wa_ref)
        weights(zb_ref, wb_ref, False)
        return 0

    lax.fori_loop(0, i // 2, pair, 0)
    values(0, wb_ref)
    o_ref[...] = acc_ref[...].T.astype(BF16)


class _CastJob(NamedTuple):
    src: jax.Array
    block: tuple
    src_index: Callable
    out_shape: tuple
    out_index: Callable


def _row_cast(src, steps):
    rows, cols = src.shape
    assert rows % (steps * 16) == 0, (rows, steps)
    return _CastJob(src, (rows // steps, cols), lambda s: (s, 0), (rows, cols), lambda s: (s, 0))


def _sb_attention(qkv, vt, batch, seq, cast_jobs, blk=SB_BLK):
    nq = seq // blk
    width = SB_GROUP * HEAD_DIM
    groups = SB_W // width
    step = lambda b, g, i: (b * groups + g) * nq + i
    at_step = lambda index: (lambda b, g, i: index(step(b, g, i)))
    outs = pl.pallas_call(
        functools.partial(_sb_kernel, blk=blk, n_casts=len(cast_jobs)),
        grid=(batch, groups, nq),
        in_specs=[
            pl.BlockSpec((blk, width), lambda b, g, i: (b * nq + i, g)),
            pl.BlockSpec((seq, width), lambda b, g, i: (b, groups + g)),
            pl.BlockSpec((1, nq, width, blk), lambda b, g, i: (b, 0, g, 0)),
        ] + [pl.BlockSpec(job.block, at_step(job.src_index)) for job in cast_jobs],
        out_specs=[pl.BlockSpec((blk, width), lambda b, g, i: (b * nq + i, g))]
        + [pl.BlockSpec(job.block, at_step(job.out_index)) for job in cast_jobs],
        out_shape=[jax.ShapeDtypeStruct((batch * seq, SB_W), BF16)]
        + [jax.ShapeDtypeStruct(job.out_shape, BF16) for job in cast_jobs],
        scratch_shapes=[
            pltpu.VMEM((blk, SB_GROUP * blk), F32),
            pltpu.VMEM((blk, SB_GROUP * blk), F32),
            pltpu.VMEM((blk, SB_GROUP * blk), BF16),
            pltpu.VMEM((blk, SB_GROUP * blk), BF16),
            pltpu.VMEM((1, SB_GROUP * blk), F32),
            pltpu.VMEM((width, blk), F32),
        ],
        compiler_params=_params("arbitrary", "arbitrary", "arbitrary"),
        name="sb_attn",
    )(qkv, qkv, vt, *[job.src for job in cast_jobs])
    return outs[0], outs[1:]


def _swa_kernel(sink_ref, q_ref, kp_ref, kc_ref, vp_ref, vc_ref, o_ref, bias_ref, *, slopes):
    blk = q_ref.shape[0]
    nk = 2 * blk
    first_step = (pl.program_id(0) == 0) & (pl.program_id(1) == 0)

    @pl.when(first_step)
    def _():
        key = lax.broadcasted_iota(jnp.int32, (nk, blk), 0)
        qry = lax.broadcasted_iota(jnp.int32, (nk, blk), 1)
        dist = qry - key + blk
        in_window = (dist >= 0) & (dist < WINDOW)
        distf = dist.astype(F32)
        for h in range(SWA_Q_HEADS):
            general = jnp.where(in_window, -(slopes[h] * distf), NEG_INF)
            bias_ref[1, :, h * blk:(h + 1) * blk] = general
            bias_ref[0, :, h * blk:(h + 1) * blk] = jnp.where(key >= blk, general, NEG_INF)

    table = jnp.minimum(pl.program_id(1), 1)
    q = q_ref[...] * BF16(1.0 / math.sqrt(HEAD_DIM))
    qt = q.astype(F32).T.astype(BF16)
    zeros = jnp.zeros((HEAD_DIM, blk), BF16)
    cols = []
    for h in range(SWA_Q_HEADS):
        qh = qt[h * HEAD_DIM:(h + 1) * HEAD_DIM]
        cols.append(jnp.concatenate([qh, zeros] if h // SWA_GROUP == 0 else [zeros, qh], axis=0))
    qstack = jnp.concatenate(cols, axis=1)
    kb = jnp.concatenate([kp_ref[...], kc_ref[...]], axis=0)
    z = jnp.dot(kb, qstack, preferred_element_type=F32) + bias_ref[table]
    sink_row = jnp.concatenate([jnp.full((1, blk), sink_ref[h], F32) for h in range(SWA_Q_HEADS)], axis=1)
    zmax = jnp.maximum(jnp.max(z, axis=0, keepdims=True), sink_row)
    p = jnp.exp((z - zmax).astype(BF16))
    sink_term = jnp.exp(sink_row - zmax)
    vt = jnp.concatenate([vp_ref[...], vc_ref[...]], axis=0).astype(F32).T
    ones = jnp.ones((HEAD_DIM, nk), F32)
    gw = SWA_GROUP * blk
    outs = []
    for g in range(SWA_KV_HEADS):
        vext = jnp.concatenate([vt[g * HEAD_DIM:(g + 1) * HEAD_DIM], ones], axis=0).astype(BF16)
        ot = jnp.dot(vext, p[:, g * gw:(g + 1) * gw], preferred_element_type=F32)
        denom = ot[HEAD_DIM:HEAD_DIM + 1, :] + sink_term[:, g * gw:(g + 1) * gw]
        res = ot[:HEAD_DIM] / denom
        outs += [res[:, k * blk:(k + 1) * blk] for k in range(SWA_GROUP)]
    o_ref[...] = jnp.concatenate(outs, axis=0).T.astype(BF16)


def _swa_attention(qkv, sinks, batch, seq, blk=128):
    nb = seq // blk
    qcol = 3 * SB_W // SWA_Q_W
    kcol = (3 * SB_W + SWA_Q_W) // SWA_KV_W
    vcol = kcol + 1
    slopes = [float(v) for v in np.power(2.0, -8.0 * np.arange(1, SWA_Q_HEADS + 1) / SWA_Q_HEADS).astype(np.float32)]
    cur = lambda col: (lambda b, n: (b * nb + n, col))
    prev = lambda col: (lambda b, n: (b * nb + jnp.maximum(n - 1, 0), col))
    return pl.pallas_call(
        functools.partial(_swa_kernel, slopes=slopes),
        grid=(batch, nb),
        in_specs=[
            pl.BlockSpec(memory_space=pltpu.SMEM),
            pl.BlockSpec((blk, SWA_Q_W), cur(qcol)),
            pl.BlockSpec((blk, SWA_KV_W), prev(kcol)),
            pl.BlockSpec((blk, SWA_KV_W), cur(kcol)),
            pl.BlockSpec((blk, SWA_KV_W), prev(vcol)),
            pl.BlockSpec((blk, SWA_KV_W), cur(vcol)),
        ],
        out_specs=pl.BlockSpec((blk, SWA_Q_W), lambda b, n: (b * nb + n, 0)),
        out_shape=jax.ShapeDtypeStruct((batch * seq, SWA_Q_W), BF16),
        scratch_shapes=[pltpu.VMEM((2, 2 * blk, SWA_Q_HEADS * blk), F32)],
        compiler_params=_params("arbitrary", "arbitrary"),
        name="swa_attn",
    )(sinks, qkv, qkv, qkv, qkv, qkv)


def _mix_kernel(xn_ref, wga_ref, wgb_ref, a_ref, wa_ref, b_ref, wb_ref, o_ref):
    xn = xn_ref[...]
    ga = jnp.dot(xn, wga_ref[...], preferred_element_type=F32)
    gb = jnp.dot(xn, wgb_ref[...], preferred_element_type=F32)
    ya = jnp.dot(a_ref[...], wa_ref[...], preferred_element_type=F32)
    yb = jnp.dot(b_ref[...], wb_ref[...], preferred_element_type=F32)
    o_ref[...] = (jax.nn.sigmoid(ga) * ya + jax.nn.sigmoid(gb) * yb).astype(BF16)


def _mix(xn, w_gates, att_a, w_a, att_b, w_b, tm=1024, tn=1024):
    t = xn.shape[0]
    nn = D_MODEL // tn
    return pl.pallas_call(
        _mix_kernel,
        grid=(t // tm, nn),
        in_specs=[
            pl.BlockSpec((tm, D_MODEL), lambda m, n: (m, 0)),
            pl.BlockSpec((D_MODEL, tn), lambda m, n: (0, n)),
            pl.BlockSpec((D_MODEL, tn), lambda m, n: (0, nn + n)),
            pl.BlockSpec((tm, SB_W), lambda m, n: (m, 0)),
            pl.BlockSpec((SB_W, tn), lambda m, n: (0, n)),
            pl.BlockSpec((tm, SWA_Q_W), lambda m, n: (m, 0)),
            pl.BlockSpec((SWA_Q_W, tn), lambda m, n: (0, n)),
        ],
        out_specs=pl.BlockSpec((tm, tn), lambda m, n: (m, n)),
        out_shape=jax.ShapeDtypeStruct((t, D_MODEL), BF16),
        compiler_params=_params("arbitrary", "arbitrary"),
        name="mix",
    )(xn, w_gates, w_gates, att_a, w_a, att_b, w_b)


def _oproj_kernel(x_ref, mixed_ref, w_ref, g_ref, h_ref, xn_ref):
    h = x_ref[...] + jnp.dot(mixed_ref[...], w_ref[...], preferred_element_type=F32)
    h_ref[...] = h
    xn_ref[...] = _rms(h, g_ref[...]).astype(BF16)


def _oproj(x2, mixed, w_o, g, tm=512):
    t = x2.shape[0]
    return pl.pallas_call(
        _oproj_kernel,
        grid=(t // tm,),
        in_specs=[
            pl.BlockSpec((tm, D_MODEL), lambda m: (m, 0)),
            pl.BlockSpec((tm, D_MODEL), lambda m: (m, 0)),
            _resident((D_MODEL, D_MODEL), lambda m: (0, 0)),
            _resident((1, D_MODEL), lambda m: (0, 0)),
        ],
        out_specs=[
            pl.BlockSpec((tm, D_MODEL), lambda m: (m, 0)),
            pl.BlockSpec((tm, D_MODEL), lambda m: (m, 0)),
        ],
        out_shape=[
            jax.ShapeDtypeStruct((t, D_MODEL), F32),
            jax.ShapeDtypeStruct((t, D_MODEL), BF16),
        ],
        compiler_params=_params("arbitrary"),
        name="oproj",
    )(x2, mixed, w_o, g)


CONV_HALO = 8
FFN_ROWS = 256


def _ffn_up_kernel(xn_ref, wg_ref, wu_ref, cg_ref, cu_ref, bg_ref, bu_ref, wd_ref, o_ref, wdb_ref, hg_ref, hu_ref,
                   wb_ref, *, tiles_per_seq):
    m = pl.program_id(1)
    tm = xn_ref.shape[0]
    wdb_ref[...] = wd_ref[...].astype(BF16)
    for h_ref in (hg_ref, hu_ref):
        @pl.when(m % tiles_per_seq == 0)
        def _():
            h_ref[0:CONV_HALO, :] = jnp.zeros((CONV_HALO, h_ref.shape[1]), F32)

        @pl.when(m % tiles_per_seq != 0)
        def _():
            h_ref[0:CONV_HALO, :] = h_ref[tm:tm + CONV_HALO, :]

    def conv(h_ref, c_ref, b_ref, r0):
        h = h_ref[r0:r0 + CONV_HALO + FFN_ROWS, :]
        out = b_ref[...] + c_ref[CONV_WIDTH - 1:CONV_WIDTH, :] * h[CONV_HALO:]
        for back in range(1, CONV_WIDTH):
            shifted = pltpu.roll(h, back, axis=0)[CONV_HALO:]
            out = out + c_ref[CONV_WIDTH - 1 - back:CONV_WIDTH - back, :] * shifted
        return out

    tn = wg_ref.shape[1]

    @pl.when(m == 0)
    def _():
        wb_ref[:, :tn] = wg_ref[...].astype(BF16)
        wb_ref[:, tn:] = wu_ref[...].astype(BF16)

    wb = wb_ref[...]
    for r0 in range(0, tm, FFN_ROWS):
        h = jnp.dot(xn_ref[r0:r0 + FFN_ROWS, :], wb, preferred_element_type=F32)
        hg_ref[CONV_HALO + r0:CONV_HALO + r0 + FFN_ROWS, :] = h[:, :tn]
        hu_ref[CONV_HALO + r0:CONV_HALO + r0 + FFN_ROWS, :] = h[:, tn:]
        gate = conv(hg_ref, cg_ref, bg_ref, r0)
        up = conv(hu_ref, cu_ref, bu_ref, r0)
        o_ref[r0:r0 + FFN_ROWS, :] = (gate * jax.nn.sigmoid(gate) * up).astype(BF16)


def _ffn_up(xn2, w_up, conv_w, conv_b, w_down, seq, tm=2048, tn=512):
    t = xn2.shape[0]
    nn = D_FF // tn
    nm = t // tm
    job = _row_cast(w_down, nn * nm)
    kern = functools.partial(_ffn_up_kernel, tiles_per_seq=seq // tm)
    return pl.pallas_call(
        kern,
        grid=(nn, nm),
        in_specs=[
            pl.BlockSpec((tm, D_MODEL), lambda n, m: (m, 0)),
            pl.BlockSpec((D_MODEL, tn), lambda n, m: (0, n)),
            pl.BlockSpec((D_MODEL, tn), lambda n, m: (0, nn + n)),
            pl.BlockSpec((CONV_WIDTH, tn), lambda n, m: (0, n)),
            pl.BlockSpec((CONV_WIDTH, tn), lambda n, m: (0, nn + n)),
            pl.BlockSpec((1, tn), lambda n, m: (0, n)),
            pl.BlockSpec((1, tn), lambda n, m: (0, nn + n)),
            pl.BlockSpec(job.block, lambda n, m: job.src_index(n * nm + m)),
        ],
        out_specs=[
            pl.BlockSpec((tm, tn), lambda n, m: (m, n)),
            pl.BlockSpec(job.block, lambda n, m: job.out_index(n * nm + m)),
        ],
        out_shape=[jax.ShapeDtypeStruct((t, D_FF), BF16), jax.ShapeDtypeStruct(job.out_shape, BF16)],
        scratch_shapes=[
            pltpu.VMEM((CONV_HALO + tm, tn), F32),
            pltpu.VMEM((CONV_HALO + tm, tn), F32),
            pltpu.VMEM((D_MODEL, 2 * tn), BF16),
        ],
        compiler_params=_params("arbitrary", "arbitrary"),
        name="ffn_up",
    )(xn2, w_up, w_up, conv_w, conv_w, conv_b, conv_b, w_down)


def _ffn_down_kernel(h_ref, act_ref, w_ref, g_ref, o_ref):
    h = h_ref[...] + jnp.dot(act_ref[...], w_ref[...], preferred_element_type=F32)
    o_ref[...] = _rms(h, g_ref[...])


def _ffn_down(h1, act, w_down, g, tm=256):
    t = h1.shape[0]
    return pl.pallas_call(
        _ffn_down_kernel,
        grid=(t // tm,),
        in_specs=[
            pl.BlockSpec((tm, D_MODEL), lambda m: (m, 0)),
            pl.BlockSpec((tm, D_FF), lambda m: (m, 0)),
            _resident((D_FF, D_MODEL), lambda m: (0, 0)),
            _resident((1, D_MODEL), lambda m: (0, 0)),
        ],
        out_specs=pl.BlockSpec((tm, D_MODEL), lambda m: (m, 0)),
        out_shape=jax.ShapeDtypeStruct((t, D_MODEL), F32),
        compiler_params=_params("arbitrary"),
        name="ffn_down",
    )(h1, act, w_down, g)


def kernel(x, norm_mix_g, w_in, w_sb_out, w_swa_out, w_o, sinks, norm_ffn_g, w_up, conv_w, conv_b, w_down, norm_final_g):
    batch, seq, _ = x.shape
    assert w_in.shape[0] == 1, "single-layer block: the final norm is fused into the FFN down projection"
    row = lambda v: v.reshape(1, -1)
    x2 = x.reshape(batch * seq, D_MODEL)
    w_in2 = w_in.reshape(D_MODEL, -1)
    xn, qkv, vt = _inproj(x2, row(norm_mix_g[0]), w_in2[:, :QKV_W].astype(BF16), batch, seq)

    sb_steps = batch * (SB_W // (SB_GROUP * HEAD_DIM)) * (seq // SB_BLK)
    gate_rows, gate_cols = 512, 256
    col_blocks = 2 * D_MODEL // gate_cols
    assert QKV_W % gate_cols == 0 and (D_MODEL // gate_rows) * col_blocks == sb_steps
    gates_job = _CastJob(
        w_in2, (gate_rows, gate_cols),
        lambda s: (s // col_blocks, QKV_W // gate_cols + s % col_blocks),
        (D_MODEL, 2 * D_MODEL), lambda s: (s // col_blocks, s % col_blocks))
    jobs = [gates_job, _row_cast(w_sb_out.reshape(SB_W, D_MODEL), sb_steps),
            _row_cast(w_swa_out.reshape(SWA_Q_W, D_MODEL), sb_steps), _row_cast(w_o.reshape(D_MODEL, D_MODEL), sb_steps)]
    att_a, (w_gates, w_a, w_b, w_ob) = _sb_attention(qkv, vt, batch, seq, jobs)
    att_b = _swa_attention(qkv, sinks[0], batch, seq)
    mixed = _mix(xn, w_gates, att_a, w_a, att_b, w_b)
    h1, xn2 = _oproj(x2, mixed, w_ob, row(norm_ffn_g[0]))
    act, w_db = _ffn_up(xn2, w_up.reshape(D_MODEL, 2 * D_FF), conv_w[0], row(conv_b[0]), w_down.reshape(D_FF, D_MODEL), seq)
    out = _ffn_down(h1, act, w_db, row(norm_final_g))
    return out.reshape(batch, seq, D_MODEL)
```

```python
import functools
import math
from typing import Callable, NamedTuple

import jax
import jax.numpy as jnp
import numpy as np
from jax import lax
from jax.experimental import pallas as pl
from jax.experimental.pallas import tpu as pltpu

D_MODEL = 2048
HEAD_DIM = 64
SB_HEADS = 16
SWA_Q_HEADS = 16
SWA_KV_HEADS = 2
SWA_GROUP = SWA_Q_HEADS // SWA_KV_HEADS
WINDOW = 128
D_FF = 5632
CONV_WIDTH = 3
RMS_EPS = 1e-5
NEG_INF = -1e30

SB_W = SB_HEADS * HEAD_DIM
SWA_Q_W = SWA_Q_HEADS * HEAD_DIM
SWA_KV_W = SWA_KV_HEADS * HEAD_DIM
QKV_W = 3 * SB_W + SWA_Q_W + 2 * SWA_KV_W
LANES = 128

BF16 = jnp.bfloat16
F32 = jnp.float32

VMEM_LIMIT = 60 * 1024 * 1024


def _params(*sem):
    return pltpu.CompilerParams(dimension_semantics=sem, vmem_limit_bytes=VMEM_LIMIT)


def _resident(shape, index_map):
    return pl.BlockSpec(shape, index_map, pipeline_mode=pl.Buffered(1))


def _rms(xf, g):
    return xf * lax.rsqrt(jnp.mean(xf * xf, axis=-1, keepdims=True) + RMS_EPS) * g


SB_BLK = 256


def _inproj_kernel(x_ref, g_ref, w_ref, xn_ref, qkv_ref, vt_ref):
    xn = _rms(x_ref[...], g_ref[...]).astype(BF16)
    xn_ref[...] = xn
    qkv = jnp.dot(xn, w_ref[...], preferred_element_type=F32)
    qkv_ref[...] = qkv.astype(BF16)
    vt = qkv[:, 2 * SB_W:3 * SB_W].T.astype(BF16)
    for kb in range(vt_ref.shape[1]):
        vt_ref[0, kb] = vt[:, kb * SB_BLK:(kb + 1) * SB_BLK]


def _inproj(x2, g, w_qkv, batch, seq, tm=512):
    t = x2.shape[0]
    tiles_per_seq = seq // tm
    return pl.pallas_call(
        _inproj_kernel,
        grid=(t // tm,),
        in_specs=[
            pl.BlockSpec((tm, D_MODEL), lambda m: (m, 0)),
            _resident((1, D_MODEL), lambda m: (0, 0)),
            _resident((D_MODEL, QKV_W), lambda m: (0, 0)),
        ],
        out_specs=[
            pl.BlockSpec((tm, D_MODEL), lambda m: (m, 0)),
            pl.BlockSpec((tm, QKV_W), lambda m: (m, 0)),
            pl.BlockSpec((1, tm // SB_BLK, SB_W, SB_BLK), lambda m: (m // tiles_per_seq, m % tiles_per_seq, 0, 0)),
        ],
        out_shape=[
            jax.ShapeDtypeStruct((t, D_MODEL), BF16),
            jax.ShapeDtypeStruct((t, QKV_W), BF16),
            jax.ShapeDtypeStruct((batch, seq // SB_BLK, SB_W, SB_BLK), BF16),
        ],
        compiler_params=_params("arbitrary"),
        name="inproj",
    )(x2, g, w_qkv)


SB_GROUP = 8
SB_CHUNK = 256


def _sb_kernel(q_ref, k_ref, vt_ref, *rest, blk, n_casts):
    cast_srcs, o_ref, cast_dsts = rest[:n_casts], rest[n_casts], rest[n_casts + 1:2 * n_casts + 1]
    za_ref, zb_ref, wa_ref, wb_ref, sa_ref, sb_ref, carry_ref, acc_ref = rest[2 * n_casts + 1:]
    wa, wb = (wa_ref, sa_ref), (wb_ref, sb_ref)
    for src, dst in zip(cast_srcs, cast_dsts):
        dst[...] = src[...].astype(BF16)
    i = pl.program_id(2)
    width = SB_GROUP * HEAD_DIM
    n = SB_GROUP * blk
    q = q_ref[...] * BF16(1.0 / math.sqrt(HEAD_DIM))
    qt = q.astype(F32).T.astype(BF16)
    chunk_heads = SB_CHUNK // HEAD_DIM
    head_of_row = lax.broadcasted_iota(jnp.int32, (SB_CHUNK, blk), 0) // HEAD_DIM
    qstacks = []
    for cc in range(width // SB_CHUNK):
        qt_c = qt[cc * SB_CHUNK:(cc + 1) * SB_CHUNK]
        qstacks.append(jnp.concatenate(
            [jnp.where(head_of_row == c, qt_c, jnp.zeros_like(qt_c)) for c in range(chunk_heads)], axis=1))
    r = lax.broadcasted_iota(jnp.int32, (blk, blk), 0)
    c = lax.broadcasted_iota(jnp.int32, (blk, blk), 1)
    later = (c > r).astype(BF16)
    key_idx = lax.broadcasted_iota(jnp.int32, (blk, n), 0)
    query_idx = lax.broadcasted_iota(jnp.int32, (blk, n), 1) & (blk - 1)
    causal = key_idx < query_idx

    def scores(j, z_ref):
        kb = k_ref[pl.ds(j * blk, blk), :]
        for cc, qs in enumerate(qstacks):
            z_ref[:, cc * chunk_heads * blk:(cc + 1) * chunk_heads * blk] = jnp.dot(
                kb[:, cc * SB_CHUNK:(cc + 1) * SB_CHUNK], qs, preferred_element_type=F32)

    def weights(z_ref, w, masked):
        a_ref, scale_ref = w
        z = z_ref[...]
        carry = carry_ref[...]
        sp = jnp.where(z > 17.0, z, jnp.log(1.0 + jnp.exp(z)))
        log_beta = z - sp
        if masked:
            sp = jnp.where(causal, sp, 0.0)
        later_sum = jnp.dot(later, sp.astype(BF16), preferred_element_type=F32)
        a = jnp.exp((log_beta - later_sum).astype(BF16))
        if masked:
            a = jnp.where(causal, a, jnp.zeros_like(a))
        a_ref[...] = a
        scale_ref[...] = jnp.exp(-carry)
        carry_ref[...] = carry + later_sum[0:1, :] + sp[0:1, :]

    def values(j, w):
        a_ref, scale_ref = w
        scale = scale_ref[...]
        for h in range(SB_GROUP):
            vt = vt_ref[0, j, h * HEAD_DIM:(h + 1) * HEAD_DIM, :]
            cols = slice(h * blk, (h + 1) * blk)
            acc_ref[h * HEAD_DIM:(h + 1) * HEAD_DIM, :] += scale[:, cols] * jnp.dot(
                vt, a_ref[:, cols], preferred_element_type=F32)

    carry_ref[...] = jnp.zeros_like(carry_ref)
    acc_ref[...] = jnp.zeros_like(acc_ref)
    first = i - 1 - i % 2

    @pl.when(i % 2 == 0)
    def _():
        scores(i, zb_ref)
        scores(jnp.maximum(first, 0), za_ref)
        weights(zb_ref, wb, True)

    @pl.when(i % 2 == 1)
    def _():
        scores(i, za_ref)
        scores(i - 1, zb_ref)
        weights(za_ref, wa, True)
        scores(jnp.maximum(first, 0), za_ref)
        values(i, wa)
        weights(zb_ref, wb, False)

    def pair(p, _):
        j = first - 2 * p
        scores(j - 1, zb_ref)
        values(j + 1, wb)
        weights(za_ref, wa, False)
        scores(jnp.maximum(j - 2, 0), za_ref)
        values(j, wa)
        weights(zb_ref, wb, False)
        return 0

    lax.fori_loop(0, i // 2, pair, 0)
    values(0, wb)
    o_ref[...] = acc_ref[...].T.astype(BF16)


class _CastJob(NamedTuple):
    src: jax.Array
    block: tuple
    src_index: Callable
    out_shape: tuple
    out_index: Callable


def _row_cast(src, steps):
    rows, cols = src.shape
    assert rows % (steps * 16) == 0, (rows, steps)
    return _CastJob(src, (rows // steps, cols), lambda s: (s, 0), (rows, cols), lambda s: (s, 0))


def _sb_attention(qkv, vt, batch, seq, cast_jobs, blk=SB_BLK):
    nq = seq // blk
    width = SB_GROUP * HEAD_DIM
    groups = SB_W // width
    step = lambda b, g, i: (b * groups + g) * nq + i
    at_step = lambda index: (lambda b, g, i: index(step(b, g, i)))
    outs = pl.pallas_call(
        functools.partial(_sb_kernel, blk=blk, n_casts=len(cast_jobs)),
        grid=(batch, groups, nq),
        in_specs=[
            pl.BlockSpec((blk, width), lambda b, g, i: (b * nq + i, g)),
            pl.BlockSpec((seq, width), lambda b, g, i: (b, groups + g)),
            pl.BlockSpec((1, nq, width, blk), lambda b, g, i: (b, 0, g, 0)),
        ] + [pl.BlockSpec(job.block, at_step(job.src_index)) for job in cast_jobs],
        out_specs=[pl.BlockSpec((blk, width), lambda b, g, i: (b * nq + i, g))]
        + [pl.BlockSpec(job.block, at_step(job.out_index)) for job in cast_jobs],
        out_shape=[jax.ShapeDtypeStruct((batch * seq, SB_W), BF16)]
        + [jax.ShapeDtypeStruct(job.out_shape, BF16) for job in cast_jobs],
        scratch_shapes=[
            pltpu.VMEM((blk, SB_GROUP * blk), F32),
            pltpu.VMEM((blk, SB_GROUP * blk), F32),
            pltpu.VMEM((blk, SB_GROUP * blk), BF16),
            pltpu.VMEM((blk, SB_GROUP * blk), BF16),
            pltpu.VMEM((1, SB_GROUP * blk), F32),
            pltpu.VMEM((1, SB_GROUP * blk), F32),
            pltpu.VMEM((1, SB_GROUP * blk), F32),
            pltpu.VMEM((width, blk), F32),
        ],
        compiler_params=_params("arbitrary", "arbitrary", "arbitrary"),
        name="sb_attn",
    )(qkv, qkv, vt, *[job.src for job in cast_jobs])
    return outs[0], outs[1:]


def _swa_kernel(sink_ref, q_ref, kp_ref, kc_ref, vp_ref, vc_ref, o_ref, bias_ref, *, slopes):
    blk = q_ref.shape[0]
    nk = 2 * blk
    first_step = (pl.program_id(0) == 0) & (pl.program_id(1) == 0)

    @pl.when(first_step)
    def _():
        key = lax.broadcasted_iota(jnp.int32, (nk, blk), 0)
        qry = lax.broadcasted_iota(jnp.int32, (nk, blk), 1)
        dist = qry - key + blk
        in_window = (dist >= 0) & (dist < WINDOW)
        distf = dist.astype(F32)
        for h in range(SWA_Q_HEADS):
            general = jnp.where(in_window, -(slopes[h] * distf), NEG_INF)
            bias_ref[1, :, h * blk:(h + 1) * blk] = general
            bias_ref[0, :, h * blk:(h + 1) * blk] = jnp.where(key >= blk, general, NEG_INF)

    table = jnp.minimum(pl.program_id(1), 1)
    q = q_ref[...] * BF16(1.0 / math.sqrt(HEAD_DIM))
    qt = q.astype(F32).T.astype(BF16)
    zeros = jnp.zeros((HEAD_DIM, blk), BF16)
    cols = []
    for h in range(SWA_Q_HEADS):
        qh = qt[h * HEAD_DIM:(h + 1) * HEAD_DIM]
        cols.append(jnp.concatenate([qh, zeros] if h // SWA_GROUP == 0 else [zeros, qh], axis=0))
    qstack = jnp.concatenate(cols, axis=1)
    kb = jnp.concatenate([kp_ref[...], kc_ref[...]], axis=0)
    z = jnp.dot(kb, qstack, preferred_element_type=F32) + bias_ref[table]
    sink_row = jnp.concatenate([jnp.full((1, blk), sink_ref[h], F32) for h in range(SWA_Q_HEADS)], axis=1)
    zmax = jnp.maximum(jnp.max(z, axis=0, keepdims=True), sink_row)
    p = jnp.exp((z - zmax).astype(BF16))
    sink_term = jnp.exp(sink_row - zmax)
    vt = jnp.concatenate([vp_ref[...], vc_ref[...]], axis=0).astype(F32).T
    ones = jnp.ones((HEAD_DIM, nk), F32)
    gw = SWA_GROUP * blk
    outs = []
    for g in range(SWA_KV_HEADS):
        vext = jnp.concatenate([vt[g * HEAD_DIM:(g + 1) * HEAD_DIM], ones], axis=0).astype(BF16)
        ot = jnp.dot(vext, p[:, g * gw:(g + 1) * gw], preferred_element_type=F32)
        denom = ot[HEAD_DIM:HEAD_DIM + 1, :] + sink_term[:, g * gw:(g + 1) * gw]
        res = ot[:HEAD_DIM] / denom
        outs += [res[:, k * blk:(k + 1) * blk] for k in range(SWA_GROUP)]
    o_ref[...] = jnp.concatenate(outs, axis=0).T.astype(BF16)


def _swa_attention(qkv, sinks, batch, seq, blk=128):
    nb = seq // blk
    qcol = 3 * SB_W // SWA_Q_W
    kcol = (3 * SB_W + SWA_Q_W) // SWA_KV_W
    vcol = kcol + 1
    slopes = [float(v) for v in np.power(2.0, -8.0 * np.arange(1, SWA_Q_HEADS + 1) / SWA_Q_HEADS).astype(np.float32)]
    cur = lambda col: (lambda b, n: (b * nb + n, col))
    prev = lambda col: (lambda b, n: (b * nb + jnp.maximum(n - 1, 0), col))
    return pl.pallas_call(
        functools.partial(_swa_kernel, slopes=slopes),
        grid=(batch, nb),
        in_specs=[
            pl.BlockSpec(memory_space=pltpu.SMEM),
            pl.BlockSpec((blk, SWA_Q_W), cur(qcol)),
            pl.BlockSpec((blk, SWA_KV_W), prev(kcol)),
            pl.BlockSpec((blk, SWA_KV_W), cur(kcol)),
            pl.BlockSpec((blk, SWA_KV_W), prev(vcol)),
            pl.BlockSpec((blk, SWA_KV_W), cur(vcol)),
        ],
        out_specs=pl.BlockSpec((blk, SWA_Q_W), lambda b, n: (b * nb + n, 0)),
        out_shape=jax.ShapeDtypeStruct((batch * seq, SWA_Q_W), BF16),
        scratch_shapes=[pltpu.VMEM((2, 2 * blk, SWA_Q_HEADS * blk), F32)],
        compiler_params=_params("arbitrary", "arbitrary"),
        name="swa_attn",
    )(sinks, qkv, qkv, qkv, qkv, qkv)


def _mix_kernel(xn_ref, wga_ref, wgb_ref, a_ref, wa_ref, b_ref, wb_ref, o_ref):
    xn = xn_ref[...]
    ga = jnp.dot(xn, wga_ref[...], preferred_element_type=F32)
    gb = jnp.dot(xn, wgb_ref[...], preferred_element_type=F32)
    ya = jnp.dot(a_ref[...], wa_ref[...], preferred_element_type=F32)
    yb = jnp.dot(b_ref[...], wb_ref[...], preferred_element_type=F32)
    o_ref[...] = (jax.nn.sigmoid(ga) * ya + jax.nn.sigmoid(gb) * yb).astype(BF16)


def _mix(xn, w_gates, att_a, w_a, att_b, w_b, tm=1024, tn=1024):
    t = xn.shape[0]
    nn = D_MODEL // tn
    return pl.pallas_call(
        _mix_kernel,
        grid=(t // tm, nn),
        in_specs=[
            pl.BlockSpec((tm, D_MODEL), lambda m, n: (m, 0)),
            pl.BlockSpec((D_MODEL, tn), lambda m, n: (0, n)),
            pl.BlockSpec((D_MODEL, tn), lambda m, n: (0, nn + n)),
            pl.BlockSpec((tm, SB_W), lambda m, n: (m, 0)),
            pl.BlockSpec((SB_W, tn), lambda m, n: (0, n)),
            pl.BlockSpec((tm, SWA_Q_W), lambda m, n: (m, 0)),
            pl.BlockSpec((SWA_Q_W, tn), lambda m, n: (0, n)),
        ],
        out_specs=pl.BlockSpec((tm, tn), lambda m, n: (m, n)),
        out_shape=jax.ShapeDtypeStruct((t, D_MODEL), BF16),
        compiler_params=_params("arbitrary", "arbitrary"),
        name="mix",
    )(xn, w_gates, w_gates, att_a, w_a, att_b, w_b)


def _oproj_kernel(x_ref, mixed_ref, w_ref, g_ref, h_ref, xn_ref):
    h = x_ref[...] + jnp.dot(mixed_ref[...], w_ref[...], preferred_element_type=F32)
    h_ref[...] = h
    xn_ref[...] = _rms(h, g_ref[...]).astype(BF16)


def _oproj(x2, mixed, w_o, g, tm=512):
    t = x2.shape[0]
    return pl.pallas_call(
        _oproj_kernel,
        grid=(t // tm,),
        in_specs=[
            pl.BlockSpec((tm, D_MODEL), lambda m: (m, 0)),
            pl.BlockSpec((tm, D_MODEL), lambda m: (m, 0)),
            _resident((D_MODEL, D_MODEL), lambda m: (0, 0)),
            _resident((1, D_MODEL), lambda m: (0, 0)),
        ],
        out_specs=[
            pl.BlockSpec((tm, D_MODEL), lambda m: (m, 0)),
            pl.BlockSpec((tm, D_MODEL), lambda m: (m, 0)),
        ],
        out_shape=[
            jax.ShapeDtypeStruct((t, D_MODEL), F32),
            jax.ShapeDtypeStruct((t, D_MODEL), BF16),
        ],
        compiler_params=_params("arbitrary"),
        name="oproj",
    )(x2, mixed, w_o, g)


CONV_HALO = 8
FFN_ROWS = 256


def _ffn_up_kernel(xn_ref, wg_ref, wu_ref, cg_ref, cu_ref, bg_ref, bu_ref, wd_ref, o_ref, wdb_ref, hg_ref, hu_ref,
                   wb_ref, *, tiles_per_seq):
    m = pl.program_id(1)
    tm = xn_ref.shape[0]
    wdb_ref[...] = wd_ref[...].astype(BF16)
    for h_ref in (hg_ref, hu_ref):
        @pl.when(m % tiles_per_seq == 0)
        def _():
            h_ref[0:CONV_HALO, :] = jnp.zeros((CONV_HALO, h_ref.shape[1]), F32)

        @pl.when(m % tiles_per_seq != 0)
        def _():
            h_ref[0:CONV_HALO, :] = h_ref[tm:tm + CONV_HALO, :]

    def conv(h_ref, c_ref, b_ref, r0):
        h = h_ref[r0:r0 + CONV_HALO + FFN_ROWS, :]
        out = b_ref[...] + c_ref[CONV_WIDTH - 1:CONV_WIDTH, :] * h[CONV_HALO:]
        for back in range(1, CONV_WIDTH):
            shifted = pltpu.roll(h, back, axis=0)[CONV_HALO:]
            out = out + c_ref[CONV_WIDTH - 1 - back:CONV_WIDTH - back, :] * shifted
        return out

    tn = wg_ref.shape[1]

    @pl.when(m == 0)
    def _():
        wb_ref[:, :tn] = wg_ref[...].astype(BF16)
        wb_ref[:, tn:] = wu_ref[...].astype(BF16)

    wb = wb_ref[...]
    for r0 in range(0, tm, FFN_ROWS):
        h = jnp.dot(xn_ref[r0:r0 + FFN_ROWS, :], wb, preferred_element_type=F32)
        hg_ref[CONV_HALO + r0:CONV_HALO + r0 + FFN_ROWS, :] = h[:, :tn]
        hu_ref[CONV_HALO + r0:CONV_HALO + r0 + FFN_ROWS, :] = h[:, tn:]
        gate = conv(hg_ref, cg_ref, bg_ref, r0)
        up = conv(hu_ref, cu_ref, bu_ref, r0)
        o_ref[r0:r0 + FFN_ROWS, :] = (gate * jax.nn.sigmoid(gate) * up).astype(BF16)


def _ffn_up(xn2, w_up, conv_w, conv_b, w_down, seq, tm=2048, tn=512):
    t = xn2.shape[0]
    nn = D_FF // tn
    nm = t // tm
    job = _row_cast(w_down, nn * nm)
    kern = functools.partial(_ffn_up_kernel, tiles_per_seq=seq // tm)
    return pl.pallas_call(
        kern,
        grid=(nn, nm),
        in_specs=[
            pl.BlockSpec((tm, D_MODEL), lambda n, m: (m, 0)),
            pl.BlockSpec((D_MODEL, tn), lambda n, m: (0, n)),
            pl.BlockSpec((D_MODEL, tn), lambda n, m: (0, nn + n)),
            pl.BlockSpec((CONV_WIDTH, tn), lambda n, m: (0, n)),
            pl.BlockSpec((CONV_WIDTH, tn), lambda n, m: (0, nn + n)),
            pl.BlockSpec((1, tn), lambda n, m: (0, n)),
            pl.BlockSpec((1, tn), lambda n, m: (0, nn + n)),
            pl.BlockSpec(job.block, lambda n, m: job.src_index(n * nm + m)),
        ],
        out_specs=[
            pl.BlockSpec((tm, tn), lambda n, m: (m, n)),
            pl.BlockSpec(job.block, lambda n, m: job.out_index(n * nm + m)),
        ],
        out_shape=[jax.ShapeDtypeStruct((t, D_FF), BF16), jax.ShapeDtypeStruct(job.out_shape, BF16)],
        scratch_shapes=[
            pltpu.VMEM((CONV_HALO + tm, tn), F32),
            pltpu.VMEM((CONV_HALO + tm, tn), F32),
            pltpu.VMEM((D_MODEL, 2 * tn), BF16),
        ],
        compiler_params=_params("arbitrary", "arbitrary"),
        name="ffn_up",
    )(xn2, w_up, w_up, conv_w, conv_w, conv_b, conv_b, w_down)


def _ffn_down_kernel(h_ref, act_ref, w_ref, g_ref, o_ref):
    h = h_ref[...] + jnp.dot(act_ref[...], w_ref[...], preferred_element_type=F32)
    o_ref[...] = _rms(h, g_ref[...])


def _ffn_down(h1, act, w_down, g, tm=256):
    t = h1.shape[0]
    return pl.pallas_call(
        _ffn_down_kernel,
        grid=(t // tm,),
        in_specs=[
            pl.BlockSpec((tm, D_MODEL), lambda m: (m, 0)),
            pl.BlockSpec((tm, D_FF), lambda m: (m, 0)),
            _resident((D_FF, D_MODEL), lambda m: (0, 0)),
            _resident((1, D_MODEL), lambda m: (0, 0)),
        ],
        out_specs=pl.BlockSpec((tm, D_MODEL), lambda m: (m, 0)),
        out_shape=jax.ShapeDtypeStruct((t, D_MODEL), F32),
        compiler_params=_params("arbitrary"),
        name="ffn_down",
    )(h1, act, w_down, g)


def kernel(x, norm_mix_g, w_in, w_sb_out, w_swa_out, w_o, sinks, norm_ffn_g, w_up, conv_w, conv_b, w_down, norm_final_g):
    batch, seq, _ = x.shape
    assert w_in.shape[0] == 1, "single-layer block: the final norm is fused into the FFN down projection"
    row = lambda v: v.reshape(1, -1)
    x2 = x.reshape(batch * seq, D_MODEL)
    w_in2 = w_in.reshape(D_MODEL, -1)
    xn, qkv, vt = _inproj(x2, row(norm_mix_g[0]), w_in2[:, :QKV_W].astype(BF16), batch, seq)

    sb_steps = batch * (SB_W // (SB_GROUP * HEAD_DIM)) * (seq // SB_BLK)
    gate_rows, gate_cols = 512, 256
    col_blocks = 2 * D_MODEL // gate_cols
    assert QKV_W % gate_cols == 0 and (D_MODEL // gate_rows) * col_blocks == sb_steps
    gates_job = _CastJob(
        w_in2, (gate_rows, gate_cols),
        lambda s: (s // col_blocks, QKV_W // gate_cols + s % col_blocks),
        (D_MODEL, 2 * D_MODEL), lambda s: (s // col_blocks, s % col_blocks))
    jobs = [gates_job, _row_cast(w_sb_out.reshape(SB_W, D_MODEL), sb_steps),
            _row_cast(w_swa_out.reshape(SWA_Q_W, D_MODEL), sb_steps), _row_cast(w_o.reshape(D_MODEL, D_MODEL), sb_steps)]
    att_a, (w_gates, w_a, w_b, w_ob) = _sb_attention(qkv, vt, batch, seq, jobs)
    att_b = _swa_attention(qkv, sinks[0], batch, seq)
    mixed = _mix(xn, w_gates, att_a, w_a, att_b, w_b)
    h1, xn2 = _oproj(x2, mixed, w_ob, row(norm_ffn_g[0]))
    act, w_db = _ffn_up(xn2, w_up.reshape(D_MODEL, 2 * D_FF), conv_w[0], row(conv_b[0]), w_down.reshape(D_FF, D_MODEL), seq)
    out = _ffn_down(h1, act, w_db, row(norm_final_g))
    return out.reshape(batch, seq, D_MODEL)
```

```python
import functools
import math
from typing import Callable, NamedTuple

import jax
import jax.numpy as jnp
import numpy as np
from jax import lax
from jax.experimental import pallas as pl
from jax.experimental.pallas import tpu as pltpu

D_MODEL = 2048
HEAD_DIM = 64
SB_HEADS = 16
SWA_Q_HEADS = 16
SWA_KV_HEADS = 2
SWA_GROUP = SWA_Q_HEADS // SWA_KV_HEADS
WINDOW = 128
D_FF = 5632
CONV_WIDTH = 3
RMS_EPS = 1e-5
NEG_INF = -1e30

SB_W = SB_HEADS * HEAD_DIM
SWA_Q_W = SWA_Q_HEADS * HEAD_DIM
SWA_KV_W = SWA_KV_HEADS * HEAD_DIM
QKV_W = 3 * SB_W + SWA_Q_W + 2 * SWA_KV_W
LANES = 128

BF16 = jnp.bfloat16
F32 = jnp.float32

VMEM_LIMIT = 60 * 1024 * 1024


def _params(*sem):
    return pltpu.CompilerParams(dimension_semantics=sem, vmem_limit_bytes=VMEM_LIMIT)


def _resident(shape, index_map):
    return pl.BlockSpec(shape, index_map, pipeline_mode=pl.Buffered(1))


def _rms(xf, g):
    return xf * lax.rsqrt(jnp.mean(xf * xf, axis=-1, keepdims=True) + RMS_EPS) * g


SB_BLK = 256


def _inproj_kernel(x_ref, g_ref, w_ref, xn_ref, qkv_ref, vt_ref):
    xn = _rms(x_ref[...], g_ref[...]).astype(BF16)
    xn_ref[...] = xn
    qkv = jnp.dot(xn, w_ref[...], preferred_element_type=F32)
    qkv_ref[...] = qkv.astype(BF16)
    vt = qkv[:, 2 * SB_W:3 * SB_W].T.astype(BF16)
    for kb in range(vt_ref.shape[1]):
        vt_ref[0, kb] = vt[:, kb * SB_BLK:(kb + 1) * SB_BLK]


def _inproj(x2, g, w_qkv, batch, seq, tm=512):
    t = x2.shape[0]
    tiles_per_seq = seq // tm
    return pl.pallas_call(
        _inproj_kernel,
        grid=(t // tm,),
        in_specs=[
            pl.BlockSpec((tm, D_MODEL), lambda m: (m, 0)),
            _resident((1, D_MODEL), lambda m: (0, 0)),
            _resident((D_MODEL, QKV_W), lambda m: (0, 0)),
        ],
        out_specs=[
            pl.BlockSpec((tm, D_MODEL), lambda m: (m, 0)),
            pl.BlockSpec((tm, QKV_W), lambda m: (m, 0)),
            pl.BlockSpec((1, tm // SB_BLK, SB_W, SB_BLK), lambda m: (m // tiles_per_seq, m % tiles_per_seq, 0, 0)),
        ],
        out_shape=[
            jax.ShapeDtypeStruct((t, D_MODEL), BF16),
            jax.ShapeDtypeStruct((t, QKV_W), BF16),
            jax.ShapeDtypeStruct((batch, seq // SB_BLK, SB_W, SB_BLK), BF16),
        ],
        compiler_params=_params("arbitrary"),
        name="inproj",
    )(x2, g, w_qkv)


SB_GROUP = 8
SB_CHUNK = 256
SOFTPLUS_LINEAR_FROM = 17.0


def _sb_kernel(q_ref, k_ref, vt_ref, *rest, blk, n_casts):
    cast_srcs, o_ref, cast_dsts = rest[:n_casts], rest[n_casts], rest[n_casts + 1:2 * n_casts + 1]
    za_ref, zb_ref, wa_ref, wb_ref, carry_ref, acc_ref = rest[2 * n_casts + 1:]
    for src, dst in zip(cast_srcs, cast_dsts):
        dst[...] = src[...].astype(BF16)
    i = pl.program_id(2)
    width = SB_GROUP * HEAD_DIM
    n = SB_GROUP * blk
    q = q_ref[...] * BF16(1.0 / math.sqrt(HEAD_DIM))
    qt = q.astype(F32).T.astype(BF16)
    chunk_heads = SB_CHUNK // HEAD_DIM
    head_of_row = lax.broadcasted_iota(jnp.int32, (SB_CHUNK, blk), 0) // HEAD_DIM
    qstacks = []
    for cc in range(width // SB_CHUNK):
        qt_c = qt[cc * SB_CHUNK:(cc + 1) * SB_CHUNK]
        qstacks.append(jnp.concatenate(
            [jnp.where(head_of_row == c, qt_c, jnp.zeros_like(qt_c)) for c in range(chunk_heads)], axis=1))
    r = lax.broadcasted_iota(jnp.int32, (blk, blk), 0)
    c = lax.broadcasted_iota(jnp.int32, (blk, blk), 1)
    later = (c > r).astype(BF16)
    key_idx = lax.broadcasted_iota(jnp.int32, (blk, n), 0)
    query_idx = lax.broadcasted_iota(jnp.int32, (blk, n), 1) & (blk - 1)
    causal = key_idx < query_idx

    def scores(j, z_ref):
        kb = k_ref[pl.ds(j * blk, blk), :]
        for cc, qs in enumerate(qstacks):
            z_ref[:, cc * chunk_heads * blk:(cc + 1) * chunk_heads * blk] = jnp.dot(
                kb[:, cc * SB_CHUNK:(cc + 1) * SB_CHUNK], qs, preferred_element_type=F32)

    def weights(z_ref, a_ref, masked):
        z = z_ref[...]
        carry = carry_ref[...]
        sp = jnp.where(z > SOFTPLUS_LINEAR_FROM, z, jnp.log(1.0 + jnp.exp(z)))
        log_beta = z - sp
        if masked:
            sp = jnp.where(causal, sp, 0.0)
        later_sum = jnp.dot(later, sp.astype(BF16), preferred_element_type=F32)
        a = jnp.exp((log_beta - later_sum - carry).astype(BF16))
        if masked:
            a = jnp.where(causal, a, jnp.zeros_like(a))
        a_ref[...] = a
        carry_ref[...] = carry + later_sum[0:1, :] + sp[0:1, :]

    def values(j, a_ref):
        for h in range(SB_GROUP):
            vt = vt_ref[0, j, h * HEAD_DIM:(h + 1) * HEAD_DIM, :]
            acc_ref[h * HEAD_DIM:(h + 1) * HEAD_DIM, :] += jnp.dot(
                vt, a_ref[:, h * blk:(h + 1) * blk], preferred_element_type=F32)

    carry_ref[...] = jnp.zeros_like(carry_ref)
    acc_ref[...] = jnp.zeros_like(acc_ref)
    first = i - 1 - i % 2

    @pl.when(i % 2 == 0)
    def _():
        scores(i, zb_ref)
        scores(jnp.maximum(first, 0), za_ref)
        weights(zb_ref, wb_ref, True)

    @pl.when(i % 2 == 1)
    def _():
        scores(i, za_ref)
        scores(i - 1, zb_ref)
        weights(za_ref, wa_ref, True)
        scores(jnp.maximum(first, 0), za_ref)
        values(i, wa_ref)
        weights(zb_ref, wb_ref, False)

    def pair(p, _):
        j = first - 2 * p
        scores(j - 1, zb_ref)
        values(j + 1, wb_ref)
        weights(za_ref, wa_ref, False)
        scores(jnp.maximum(j - 2, 0), za_ref)
        values(j, wa_ref)
        weights(zb_ref, wb_ref, False)
        return 0

    lax.fori_loop(0, i // 2, pair, 0)
    values(0, wb_ref)
    o_ref[...] = acc_ref[...].T.astype(BF16)


class _CastJob(NamedTuple):
    src: jax.Array
    block: tuple
    src_index: Callable
    out_shape: tuple
    out_index: Callable


def _row_cast(src, steps):
    rows, cols = src.shape
    assert rows % (steps * 16) == 0, (rows, steps)
    return _CastJob(src, (rows // steps, cols), lambda s: (s, 0), (rows, cols), lambda s: (s, 0))


def _sb_attention(qkv, vt, batch, seq, cast_jobs, blk=SB_BLK):
    nq = seq // blk
    width = SB_GROUP * HEAD_DIM
    groups = SB_W // width
    step = lambda b, g, i: (b * groups + g) * nq + i
    at_step = lambda index: (lambda b, g, i: index(step(b, g, i)))
    outs = pl.pallas_call(
        functools.partial(_sb_kernel, blk=blk, n_casts=len(cast_jobs)),
        grid=(batch, groups, nq),
        in_specs=[
            pl.BlockSpec((blk, width), lambda b, g, i: (b * nq + i, g)),
            pl.BlockSpec((seq, width), lambda b, g, i: (b, groups + g)),
            pl.BlockSpec((1, nq, width, blk), lambda b, g, i: (b, 0, g, 0)),
        ] + [pl.BlockSpec(job.block, at_step(job.src_index)) for job in cast_jobs],
        out_specs=[pl.BlockSpec((blk, width), lambda b, g, i: (b * nq + i, g))]
        + [pl.BlockSpec(job.block, at_step(job.out_index)) for job in cast_jobs],
        out_shape=[jax.ShapeDtypeStruct((batch * seq, SB_W), BF16)]
        + [jax.ShapeDtypeStruct(job.out_shape, BF16) for job in cast_jobs],
        scratch_shapes=[
            pltpu.VMEM((blk, SB_GROUP * blk), F32),
            pltpu.VMEM((blk, SB_GROUP * blk), F32),
            pltpu.VMEM((blk, SB_GROUP * blk), BF16),
            pltpu.VMEM((blk, SB_GROUP * blk), BF16),
            pltpu.VMEM((1, SB_GROUP * blk), F32),
            pltpu.VMEM((width, blk), F32),
        ],
        compiler_params=_params("arbitrary", "arbitrary", "arbitrary"),
        name="sb_attn",
    )(qkv, qkv, vt, *[job.src for job in cast_jobs])
    return outs[0], outs[1:]


def _swa_kernel(sink_ref, q_ref, kp_ref, kc_ref, vp_ref, vc_ref, o_ref, bias_ref, *, slopes):
    blk = q_ref.shape[0]
    nk = 2 * blk
    first_step = (pl.program_id(0) == 0) & (pl.program_id(1) == 0)

    @pl.when(first_step)
    def _():
        key = lax.broadcasted_iota(jnp.int32, (nk, blk), 0)
        qry = lax.broadcasted_iota(jnp.int32, (nk, blk), 1)
        dist = qry - key + blk
        in_window = (dist >= 0) & (dist < WINDOW)
        distf = dist.astype(F32)
        for h in range(SWA_Q_HEADS):
            general = jnp.where(in_window, -(slopes[h] * distf), NEG_INF)
            bias_ref[1, :, h * blk:(h + 1) * blk] = general
            bias_ref[0, :, h * blk:(h + 1) * blk] = jnp.where(key >= blk, general, NEG_INF)

    table = jnp.minimum(pl.program_id(1), 1)
    q = q_ref[...] * BF16(1.0 / math.sqrt(HEAD_DIM))
    qt = q.astype(F32).T.astype(BF16)
    zeros = jnp.zeros((HEAD_DIM, blk), BF16)
    cols = []
    for h in range(SWA_Q_HEADS):
        qh = qt[h * HEAD_DIM:(h + 1) * HEAD_DIM]
        cols.append(jnp.concatenate([qh, zeros] if h // SWA_GROUP == 0 else [zeros, qh], axis=0))
    qstack = jnp.concatenate(cols, axis=1)
    kb = jnp.concatenate([kp_ref[...], kc_ref[...]], axis=0)
    z = jnp.dot(kb, qstack, preferred_element_type=F32) + bias_ref[table]
    sink_row = jnp.concatenate([jnp.full((1, blk), sink_ref[h], F32) for h in range(SWA_Q_HEADS)], axis=1)
    zmax = jnp.maximum(jnp.max(z, axis=0, keepdims=True), sink_row)
    p = jnp.exp((z - zmax).astype(BF16))
    sink_term = jnp.exp(sink_row - zmax)
    vt = jnp.concatenate([vp_ref[...], vc_ref[...]], axis=0).astype(F32).T
    ones = jnp.ones((HEAD_DIM, nk), F32)
    gw = SWA_GROUP * blk
    outs = []
    for g in range(SWA_KV_HEADS):
        vext = jnp.concatenate([vt[g * HEAD_DIM:(g + 1) * HEAD_DIM], ones], axis=0).astype(BF16)
        ot = jnp.dot(vext, p[:, g * gw:(g + 1) * gw], preferred_element_type=F32)
        denom = ot[HEAD_DIM:HEAD_DIM + 1, :] + sink_term[:, g * gw:(g + 1) * gw]
        res = ot[:HEAD_DIM] / denom
        outs += [res[:, k * blk:(k + 1) * blk] for k in range(SWA_GROUP)]
    o_ref[...] = jnp.concatenate(outs, axis=0).T.astype(BF16)


def _swa_attention(qkv, sinks, batch, seq, blk=128):
    nb = seq // blk
    qcol = 3 * SB_W // SWA_Q_W
    kcol = (3 * SB_W + SWA_Q_W) // SWA_KV_W
    vcol = kcol + 1
    slopes = [float(v) for v in np.power(2.0, -8.0 * np.arange(1, SWA_Q_HEADS + 1) / SWA_Q_HEADS).astype(np.float32)]
    cur = lambda col: (lambda b, n: (b * nb + n, col))
    prev = lambda col: (lambda b, n: (b * nb + jnp.maximum(n - 1, 0), col))
    return pl.pallas_call(
        functools.partial(_swa_kernel, slopes=slopes),
        grid=(batch, nb),
        in_specs=[
            pl.BlockSpec(memory_space=pltpu.SMEM),
            pl.BlockSpec((blk, SWA_Q_W), cur(qcol)),
            pl.BlockSpec((blk, SWA_KV_W), prev(kcol)),
            pl.BlockSpec((blk, SWA_KV_W), cur(kcol)),
            pl.BlockSpec((blk, SWA_KV_W), prev(vcol)),
            pl.BlockSpec((blk, SWA_KV_W), cur(vcol)),
        ],
        out_specs=pl.BlockSpec((blk, SWA_Q_W), lambda b, n: (b * nb + n, 0)),
        out_shape=jax.ShapeDtypeStruct((batch * seq, SWA_Q_W), BF16),
        scratch_shapes=[pltpu.VMEM((2, 2 * blk, SWA_Q_HEADS * blk), F32)],
        compiler_params=_params("arbitrary", "arbitrary"),
        name="swa_attn",
    )(sinks, qkv, qkv, qkv, qkv, qkv)


def _mix_kernel(xn_ref, wga_ref, wgb_ref, a_ref, wa_ref, b_ref, wb_ref, o_ref):
    xn = xn_ref[...]
    ga = jnp.dot(xn, wga_ref[...], preferred_element_type=F32)
    gb = jnp.dot(xn, wgb_ref[...], preferred_element_type=F32)
    ya = jnp.dot(a_ref[...], wa_ref[...], preferred_element_type=F32)
    yb = jnp.dot(b_ref[...], wb_ref[...], preferred_element_type=F32)
    o_ref[...] = (jax.nn.sigmoid(ga) * ya + jax.nn.sigmoid(gb) * yb).astype(BF16)


def _mix(xn, w_gates, att_a, w_a, att_b, w_b, tm=1024, tn=1024):
    t = xn.shape[0]
    nn = D_MODEL // tn
    return pl.pallas_call(
        _mix_kernel,
        grid=(t // tm, nn),
        in_specs=[
            pl.BlockSpec((tm, D_MODEL), lambda m, n: (m, 0)),
            pl.BlockSpec((D_MODEL, tn), lambda m, n: (0, n)),
            pl.BlockSpec((D_MODEL, tn), lambda m, n: (0, nn + n)),
            pl.BlockSpec((tm, SB_W), lambda m, n: (m, 0)),
            pl.BlockSpec((SB_W, tn), lambda m, n: (0, n)),
            pl.BlockSpec((tm, SWA_Q_W), lambda m, n: (m, 0)),
            pl.BlockSpec((SWA_Q_W, tn), lambda m, n: (0, n)),
        ],
        out_specs=pl.BlockSpec((tm, tn), lambda m, n: (m, n)),
        out_shape=jax.ShapeDtypeStruct((t, D_MODEL), BF16),
        compiler_params=_params("arbitrary", "arbitrary"),
        name="mix",
    )(xn, w_gates, w_gates, att_a, w_a, att_b, w_b)


def _oproj_kernel(x_ref, mixed_ref, w_ref, g_ref, h_ref, xn_ref):
    h = x_ref[...] + jnp.dot(mixed_ref[...], w_ref[...], preferred_element_type=F32)
    h_ref[...] = h
    xn_ref[...] = _rms(h, g_ref[...]).astype(BF16)


def _oproj(x2, mixed, w_o, g, tm=512):
    t = x2.shape[0]
    return pl.pallas_call(
        _oproj_kernel,
        grid=(t // tm,),
        in_specs=[
            pl.BlockSpec((tm, D_MODEL), lambda m: (m, 0)),
            pl.BlockSpec((tm, D_MODEL), lambda m: (m, 0)),
            _resident((D_MODEL, D_MODEL), lambda m: (0, 0)),
            _resident((1, D_MODEL), lambda m: (0, 0)),
        ],
        out_specs=[
            pl.BlockSpec((tm, D_MODEL), lambda m: (m, 0)),
            pl.BlockSpec((tm, D_MODEL), lambda m: (m, 0)),
        ],
        out_shape=[
            jax.ShapeDtypeStruct((t, D_MODEL), F32),
            jax.ShapeDtypeStruct((t, D_MODEL), BF16),
        ],
        compiler_params=_params("arbitrary"),
        name="oproj",
    )(x2, mixed, w_o, g)


CONV_HALO = 8
FFN_ROWS = 256


def _ffn_up_kernel(xn_ref, wg_ref, wu_ref, cg_ref, cu_ref, bg_ref, bu_ref, wd_ref, o_ref, wdb_ref, hg_ref, hu_ref,
                   wb_ref, *, tiles_per_seq):
    m = pl.program_id(1)
    tm = xn_ref.shape[0]
    wdb_ref[...] = wd_ref[...].astype(BF16)
    for h_ref in (hg_ref, hu_ref):
        @pl.when(m % tiles_per_seq == 0)
        def _():
            h_ref[0:CONV_HALO, :] = jnp.zeros((CONV_HALO, h_ref.shape[1]), F32)

        @pl.when(m % tiles_per_seq != 0)
        def _():
            h_ref[0:CONV_HALO, :] = h_ref[tm:tm + CONV_HALO, :]

    def conv(h_ref, c_ref, b_ref, r0):
        h = h_ref[r0:r0 + CONV_HALO + FFN_ROWS, :]
        out = b_ref[...] + c_ref[CONV_WIDTH - 1:CONV_WIDTH, :] * h[CONV_HALO:]
        for back in range(1, CONV_WIDTH):
            shifted = pltpu.roll(h, back, axis=0)[CONV_HALO:]
            out = out + c_ref[CONV_WIDTH - 1 - back:CONV_WIDTH - back, :] * shifted
        return out

    tn = wg_ref.shape[1]

    @pl.when(m == 0)
    def _():
        wb_ref[:, :tn] = wg_ref[...].astype(BF16)
        wb_ref[:, tn:] = wu_ref[...].astype(BF16)

    wb = wb_ref[...]
    for r0 in range(0, tm, FFN_ROWS):
        h = jnp.dot(xn_ref[r0:r0 + FFN_ROWS, :], wb, preferred_element_type=F32)
        hg_ref[CONV_HALO + r0:CONV_HALO + r0 + FFN_ROWS, :] = h[:, :tn]
        hu_ref[CONV_HALO + r0:CONV_HALO + r0 + FFN_ROWS, :] = h[:, tn:]
        gate = conv(hg_ref, cg_ref, bg_ref, r0)
        up = conv(hu_ref, cu_ref, bu_ref, r0)
        o_ref[r0:r0 + FFN_ROWS, :] = (gate * jax.nn.sigmoid(gate) * up).astype(BF16)


def _ffn_up(xn2, w_up, conv_w, conv_b, w_down, seq, tm=2048, tn=512):
    t = xn2.shape[0]
    nn = D_FF // tn
    nm = t // tm
    job = _row_cast(w_down, nn * nm)
    kern = functools.partial(_ffn_up_kernel, tiles_per_seq=seq // tm)
    return pl.pallas_call(
        kern,
        grid=(nn, nm),
        in_specs=[
            pl.BlockSpec((tm, D_MODEL), lambda n, m: (m, 0)),
            pl.BlockSpec((D_MODEL, tn), lambda n, m: (0, n)),
            pl.BlockSpec((D_MODEL, tn), lambda n, m: (0, nn + n)),
            pl.BlockSpec((CONV_WIDTH, tn), lambda n, m: (0, n)),
            pl.BlockSpec((CONV_WIDTH, tn), lambda n, m: (0, nn + n)),
            pl.BlockSpec((1, tn), lambda n, m: (0, n)),
            pl.BlockSpec((1, tn), lambda n, m: (0, nn + n)),
            pl.BlockSpec(job.block, lambda n, m: job.src_index(n * nm + m)),
        ],
        out_specs=[
            pl.BlockSpec((tm, tn), lambda n, m: (m, n)),
            pl.BlockSpec(job.block, lambda n, m: job.out_index(n * nm + m)),
        ],
        out_shape=[jax.ShapeDtypeStruct((t, D_FF), BF16), jax.ShapeDtypeStruct(job.out_shape, BF16)],
        scratch_shapes=[
            pltpu.VMEM((CONV_HALO + tm, tn), F32),
            pltpu.VMEM((CONV_HALO + tm, tn), F32),
            pltpu.VMEM((D_MODEL, 2 * tn), BF16),
        ],
        compiler_params=_params("arbitrary", "arbitrary"),
        name="ffn_up",
    )(xn2, w_up, w_up, conv_w, conv_w, conv_b, conv_b, w_down)


def _ffn_down_kernel(h_ref, act_ref, w_ref, g_ref, o_ref):
    h = h_ref[...] + jnp.dot(act_ref[...], w_ref[...], preferred_element_type=F32)
    o_ref[...] = _rms(h, g_ref[...])


def _ffn_down(h1, act, w_down, g, tm=256):
    t = h1.shape[0]
    return pl.pallas_call(
        _ffn_down_kernel,
        grid=(t // tm,),
        in_specs=[
            pl.BlockSpec((tm, D_MODEL), lambda m: (m, 0)),
            pl.BlockSpec((tm, D_FF), lambda m: (m, 0)),
            _resident((D_FF, D_MODEL), lambda m: (0, 0)),
            _resident((1, D_MODEL), lambda m: (0, 0)),
        ],
        out_specs=pl.BlockSpec((tm, D_MODEL), lambda m: (m, 0)),
        out_shape=jax.ShapeDtypeStruct((t, D_MODEL), F32),
        compiler_params=_params("arbitrary"),
        name="ffn_down",
    )(h1, act, w_down, g)


def kernel(x, norm_mix_g, w_in, w_sb_out, w_swa_out, w_o, sinks, norm_ffn_g, w_up, conv_w, conv_b, w_down, norm_final_g):
    batch, seq, _ = x.shape
    assert w_in.shape[0] == 1, "single-layer block: the final norm is fused into the FFN down projection"
    row = lambda v: v.reshape(1, -1)
    x2 = x.reshape(batch * seq, D_MODEL)
    w_in2 = w_in.reshape(D_MODEL, -1)
    xn, qkv, vt = _inproj(x2, row(norm_mix_g[0]), w_in2[:, :QKV_W].astype(BF16), batch, seq)

    sb_steps = batch * (SB_W // (SB_GROUP * HEAD_DIM)) * (seq // SB_BLK)
    gate_rows, gate_cols = 512, 256
    col_blocks = 2 * D_MODEL // gate_cols
    assert QKV_W % gate_cols == 0 and (D_MODEL // gate_rows) * col_blocks == sb_steps
    gates_job = _CastJob(
        w_in2, (gate_rows, gate_cols),
        lambda s: (s // col_blocks, QKV_W // gate_cols + s % col_blocks),
        (D_MODEL, 2 * D_MODEL), lambda s: (s // col_blocks, s % col_blocks))
    jobs = [gates_job, _row_cast(w_sb_out.reshape(SB_W, D_MODEL), sb_steps),
            _row_cast(w_swa_out.reshape(SWA_Q_W, D_MODEL), sb_steps), _row_cast(w_o.reshape(D_MODEL, D_MODEL), sb_steps)]
    att_a, (w_gates, w_a, w_b, w_ob) = _sb_attention(qkv, vt, batch, seq, jobs)
    att_b = _swa_attention(qkv, sinks[0], batch, seq)
    mixed = _mix(xn, w_gates, att_a, w_a, att_b, w_b)
    h1, xn2 = _oproj(x2, mixed, w_ob, row(norm_ffn_g[0]))
    act, w_db = _ffn_up(xn2, w_up.reshape(D_MODEL, 2 * D_FF), conv_w[0], row(conv_b[0]), w_down.reshape(D_FF, D_MODEL), seq)
    out = _ffn_down(h1, act, w_db, row(norm_final_g))
    return out.reshape(batch, seq, D_MODEL)
```
